```python
import math
import jax
import jax.numpy as jnp
from jax import lax
import numpy as np

D_MODEL = 2048
BATCH = 2
SEQ = 4096
DEPTH = 4
DEC_BATCH = 8
DEC_SEQ = 1
PAST_LEN = 16384
PAGE_SIZE = 128

HEAD_DIM = 128
H_A = D_MODEL // (2 * HEAD_DIM)
DK_A = HEAD_DIM
DV_A = HEAD_DIM
W_A = H_A * DV_A
C_CONV = 2 * H_A * DK_A + H_A * DV_A
CONV_W = 4
CHUNK = 64
H_B = D_MODEL // (2 * HEAD_DIM)
HD_B = HEAD_DIM
W_B = H_B * HD_B
Q_BLOCK = 128
N_GROUPS = 4
EXPERTS_PER_GROUP = 4
N_EXPERTS = N_GROUPS * EXPERTS_PER_GROUP
TOP_K = 2
D_EXPERT = D_MODEL // 4
ALPHA = (2 * DEPTH) ** 0.25
BETA = (8 * DEPTH) ** -0.25
LN_EPS = 1e-5
RMS_EPS = 1e-6
L2_EPS = 1e-6

_SPLIT_SIZES = (C_CONV, W_A, H_A, H_A, W_B, W_B, W_B, H_B, D_MODEL, D_MODEL)
N_IN = sum(_SPLIT_SIZES)
SPLIT_POINTS = tuple(int(v) for v in np.cumsum(_SPLIT_SIZES)[:-1])

kernel_name = 'hybrid_deltanet_fox_hmoe_deepnorm_step'


def layer_norm(x, g, b):
    xf = x.astype(jnp.float32)
    mu = jnp.mean(xf, axis=-1, keepdims=True)
    var = jnp.mean(jnp.square(xf - mu), axis=-1, keepdims=True)
    return ((xf - mu) * lax.rsqrt(var + LN_EPS) * g + b).astype(x.dtype)


def rms_norm(x):
    xf = x.astype(jnp.float32)
    return xf * lax.rsqrt(jnp.mean(jnp.square(xf), axis=-1, keepdims=True) + RMS_EPS)


def l2_normalize(x):
    xf = x.astype(jnp.float32)
    return xf * lax.rsqrt(jnp.sum(jnp.square(xf), axis=-1, keepdims=True) + L2_EPS)


def causal_short_conv(x, buf, w):
    xx = jnp.concatenate([buf.astype(x.dtype), x], axis=1)
    out = lax.conv_general_dilated(
        xx, w[:, None, :].astype(x.dtype), window_strides=(1,), padding='VALID',
        dimension_numbers=('NWC', 'WIO', 'NWC'), feature_group_count=x.shape[-1])
    return jax.nn.silu(out), xx[:, -(CONV_W - 1):]


def gated_delta_chunked(q, k, v, g, beta, s0):
    f32 = jnp.float32
    B, L, H, DK = q.shape
    DV = v.shape[-1]
    C = math.gcd(CHUNK, L)
    N = L // C

    def chunks(t):
        return t.astype(f32).reshape(B, N, C, H, -1).transpose(0, 1, 3, 2, 4)

    qc, kc, vc = chunks(q), chunks(k), chunks(v)
    gc = g.astype(f32).reshape(B, N, C, H).transpose(0, 1, 3, 2)
    bc = beta.astype(f32).reshape(B, N, C, H).transpose(0, 1, 3, 2)
    G = jnp.cumsum(gc, axis=-1)
    tri_incl = jnp.tril(jnp.ones((C, C), dtype=bool))
    tri_strict = jnp.tril(jnp.ones((C, C), dtype=bool), -1)
    diff = G[..., :, None] - G[..., None, :]
    decay = jnp.where(tri_incl, jnp.exp(jnp.where(tri_incl, diff, 0.0)), 0.0)
    kk = jnp.einsum('bnhtd,bnhid->bnhti', kc, kc)
    a_mat = jnp.where(tri_strict, bc[..., :, None] * decay * kk, 0.0) + jnp.eye(C, dtype=f32)
    rhs = jnp.concatenate([(bc * jnp.exp(G))[..., None] * kc, bc[..., None] * vc], axis=-1)
    sol = lax.linalg.triangular_solve(a_mat, rhs, left_side=True, lower=True, unit_diagonal=True)
    w_blk, uv_blk = sol[..., :DK], sol[..., DK:]
    qk = jnp.einsum('bnhtd,bnhid->bnhti', qc, kc) * decay
    q_dec = qc * jnp.exp(G)[..., None]
    k_dec = kc * jnp.exp(G[..., -1:] - G)[..., None]
    g_end = jnp.exp(G[..., -1])

    def step(S, inp):
        w_n, uv_n, qk_n, qd_n, kd_n, ge_n = inp
        u = uv_n - jnp.einsum('bhcd,bhde->bhce', w_n, S)
        o = jnp.einsum('bhcd,bhde->bhce', qd_n, S) + jnp.einsum('bhti,bhie->bhte', qk_n, u)
        S = ge_n[..., None, None] * S + jnp.einsum('bhcd,bhce->bhde', kd_n, u)
        return S, o

    xs = tuple(jnp.moveaxis(t, 1, 0) for t in (w_blk, uv_blk, qk, q_dec, k_dec, g_end))
    s_final, o = lax.scan(step, s0.astype(f32), xs)
    o = o.transpose(1, 0, 3, 2, 4).reshape(B, L, H, DV)
    return o, s_final


def forgetting_attention(q, k, v, f_q, f_k, q_start):
    B, Lq, H, D = q.shape
    Lk = k.shape[1]
    qb = math.gcd(Q_BLOCK, Lq)
    nb = Lq // qb
    q_blk = q.reshape(B, nb, qb, H, D).transpose(1, 0, 2, 3, 4)
    fq_blk = f_q.reshape(B, nb, qb, H).transpose(1, 0, 3, 2)
    pos_blk = (q_start + jnp.arange(Lq)).reshape(nb, qb)
    fk_t = f_k.transpose(0, 2, 1)
    k_pos = jnp.arange(Lk)
    scale = D ** -0.5

    def block(args):
        qi, fqi, pi = args
        s = jnp.einsum('bqhd,bkhd->bhqk', qi, k, preferred_element_type=jnp.float32) * scale
        s = s + fqi[..., None] - fk_t[:, :, None, :]
        s = jnp.where(pi[:, None] >= k_pos[None, :], s, -jnp.inf)
        p = jax.nn.softmax(s, axis=-1)
        return jnp.einsum('bhqk,bkhd->bqhd', p.astype(v.dtype), v)

    o = lax.map(block, (q_blk, fq_blk, pos_blk))
    return o.transpose(1, 0, 2, 3, 4).reshape(B, Lq, H, D)


def token_mixer(x, w_in, conv_w, a_log, dt_bias, delta_norm_w, forget_b, w_ba, w_bb, w_o,
                conv_buf, s0, past_k, past_v, past_logf):
    B, L, _ = x.shape
    P = past_k.shape[1]
    f32 = jnp.float32
    proj = x @ w_in
    (qkv_a, z_a, beta_a, dec_a, q_b, k_b, v_b, f_b, gate_a, gate_b) = jnp.split(proj, SPLIT_POINTS, axis=-1)

    qkv_a, new_buf = causal_short_conv(qkv_a, conv_buf, conv_w)
    q_a, k_a, v_a = jnp.split(qkv_a, [H_A * DK_A, 2 * H_A * DK_A], axis=-1)
    q_a = l2_normalize(q_a.reshape(B, L, H_A, DK_A)) * (DK_A ** -0.5)
    k_a = l2_normalize(k_a.reshape(B, L, H_A, DK_A))
    v_a = v_a.reshape(B, L, H_A, DV_A)
    beta = jax.nn.sigmoid(beta_a.astype(f32))
    g = -jnp.exp(a_log.astype(f32)) * jax.nn.softplus(dec_a.astype(f32) + dt_bias)
    o_a, s_new = gated_delta_chunked(q_a, k_a, v_a, g, beta, s0)
    o_a = rms_norm(o_a) * delta_norm_w * jax.nn.silu(z_a.reshape(B, L, H_A, DV_A).astype(f32))
    branch_a = o_a.reshape(B, L, W_A).astype(x.dtype) @ w_ba

    q_b = q_b.reshape(B, L, H_B, HD_B)
    k_b = k_b.reshape(B, L, H_B, HD_B)
    v_b = v_b.reshape(B, L, H_B, HD_B)
    logf = jax.nn.log_sigmoid(f_b.astype(f32) + forget_b)
    k_all = jnp.concatenate([past_k.astype(x.dtype), k_b], axis=1)
    v_all = jnp.concatenate([past_v.astype(x.dtype), v_b], axis=1)
    F = jnp.cumsum(jnp.concatenate([past_logf.astype(f32), logf], axis=1), axis=1)
    o_b = forgetting_attention(q_b, k_all, v_all, F[:, P:], F, P)
    branch_b = o_b.reshape(B, L, W_B) @ w_bb

    merged = jax.nn.sigmoid(gate_a) * branch_a + jax.nn.sigmoid(gate_b) * branch_b
    return merged @ w_o, (k_b, v_b, logf, new_buf, s_new)


def hierarchical_moe(x, rg_w, rg_b, re_w, re_b, w_gate, w_up, w_down):
    T = x.shape[0]
    lg = (x @ rg_w + rg_b).astype(jnp.float32)
    pg = jax.nn.softmax(lg, axis=-1)
    gi = jnp.argmax(lg, axis=-1)
    wg = jnp.take_along_axis(pg, gi[:, None], axis=1)
    le = (x @ re_w + re_b).astype(jnp.float32).reshape(T, N_GROUPS, EXPERTS_PER_GROUP)
    le = jnp.take_along_axis(le, gi[:, None, None], axis=1)[:, 0]
    tv, ti = lax.top_k(le, TOP_K)
    tw = jax.nn.softmax(tv, axis=-1) * wg
    eidx = gi[:, None] * EXPERTS_PER_GROUP + ti
    gate = jnp.einsum('tk,tke->te', tw, jax.nn.one_hot(eidx, N_EXPERTS, dtype=jnp.float32))
    h = jax.nn.silu(jnp.einsum('td,edf->tef', x, w_gate)) * jnp.einsum('td,edf->tef', x, w_up)
    return jnp.einsum('tef,efd->td', h * gate[:, :, None].astype(h.dtype), w_down)


def decoder_layer(x, mix_w, moe_w, norm_w, conv_buf, s0, past_k, past_v, past_logf):
    m, new_state = token_mixer(x, *mix_w, conv_buf, s0, past_k, past_v, past_logf)
    x = layer_norm(ALPHA * x + m, norm_w[0], norm_w[1])
    B, L, D = x.shape
    f = hierarchical_moe(x.reshape(B * L, D), *moe_w).reshape(B, L, D)
    x = layer_norm(ALPHA * x + f, norm_w[2], norm_w[3])
    return x, new_state


def setup_inputs(seed: int = 0) -> dict:
    key = jax.random.key(seed)
    ks = iter(jax.random.split(key, 40))
    f32 = jnp.float32
    n_pages = PAST_LEN // PAGE_SIZE
    n_used = DEC_BATCH * n_pages
    n_pool = n_used + max(1, n_used // 4)

    def nrm(shape, scale):
        return jax.random.normal(next(ks), shape, f32) * scale

    x_prompt = nrm((BATCH, SEQ, D_MODEL), 1.0)
    x_sample = nrm((DEC_BATCH, DEC_SEQ, D_MODEL), 1.0)
    cache_k = nrm((n_pool, DEPTH, PAGE_SIZE, H_B, HD_B), 1.0)
    cache_v = nrm((n_pool, DEPTH, PAGE_SIZE, H_B, HD_B), 1.0)
    cache_logf = jax.nn.log_sigmoid(nrm((n_pool, DEPTH, PAGE_SIZE, H_B), 1.0) + 4.0)
    state_conv = nrm((DEC_BATCH, DEPTH, CONV_W - 1, C_CONV), 1.0)
    state_delta = nrm((DEC_BATCH, DEPTH, H_A, DK_A, DV_A), DK_A ** -0.5)
    page_table = jax.random.permutation(next(ks), n_pool)[:n_used].reshape(DEC_BATCH, n_pages).astype(jnp.int32)

    w_in = nrm((DEPTH, D_MODEL, N_IN), D_MODEL ** -0.5)
    conv_w = nrm((DEPTH, CONV_W, C_CONV), CONV_W ** -0.5)
    a_log = jnp.log(jax.random.uniform(next(ks), (DEPTH, H_A), f32, 1.0, 16.0))
    dt = jnp.exp(jax.random.uniform(next(ks), (DEPTH, H_A), f32, math.log(1e-3), math.log(1e-1)))
    dt_bias = dt + jnp.log(-jnp.expm1(-dt))
    delta_norm_w = 1.0 + nrm((DEPTH, DV_A), 0.02)
    fox_forget_b = jax.random.uniform(next(ks), (DEPTH, H_B), f32, 1.0, 6.0)
    w_branch_a = nrm((DEPTH, W_A, D_MODEL), W_A ** -0.5)
    w_branch_b = nrm((DEPTH, W_B, D_MODEL), W_B ** -0.5)
    w_out = nrm((DEPTH, D_MODEL, D_MODEL), BETA * D_MODEL ** -0.5)
    ln1_g = 1.0 + nrm((DEPTH, D_MODEL), 0.02)
    ln1_b = nrm((DEPTH, D_MODEL), 0.02)
    router_group_w = nrm((DEPTH, D_MODEL, N_GROUPS), D_MODEL ** -0.5)
    router_group_b = nrm((DEPTH, N_GROUPS), 0.01)
    router_expert_w = nrm((DEPTH, D_MODEL, N_EXPERTS), D_MODEL ** -0.5)
    router_expert_b = nrm((DEPTH, N_EXPERTS), 0.01)
    expert_w_gate = nrm((DEPTH, N_EXPERTS, D_MODEL, D_EXPERT), D_MODEL ** -0.5)
    expert_w_up = nrm((DEPTH, N_EXPERTS, D_MODEL, D_EXPERT), D_MODEL ** -0.5)
    expert_w_down = nrm((DEPTH, N_EXPERTS, D_EXPERT, D_MODEL), BETA * D_EXPERT ** -0.5)
    ln2_g = 1.0 + nrm((DEPTH, D_MODEL), 0.02)
    ln2_b = nrm((DEPTH, D_MODEL), 0.02)
    return {
        'x_prompt': x_prompt, 'x_sample': x_sample,
        'cache_k': cache_k, 'cache_v': cache_v, 'cache_logf': cache_logf,
        'state_conv': state_conv, 'state_delta': state_delta, 'page_table': page_table,
        'w_in': w_in, 'conv_w': conv_w, 'a_log': a_log, 'dt_bias': dt_bias,
        'delta_norm_w': delta_norm_w, 'fox_forget_b': fox_forget_b,
        'w_branch_a': w_branch_a, 'w_branch_b': w_branch_b, 'w_out': w_out,
        'ln1_g': ln1_g, 'ln1_b': ln1_b,
        'router_group_w': router_group_w, 'router_group_b': router_group_b,
        'router_expert_w': router_expert_w, 'router_expert_b': router_expert_b,
        'expert_w_gate': expert_w_gate, 'expert_w_up': expert_w_up, 'expert_w_down': expert_w_down,
        'ln2_g': ln2_g, 'ln2_b': ln2_b,
    }


def reference(x_prompt, x_sample, cache_k, cache_v, cache_logf, state_conv, state_delta, page_table,
              w_in, conv_w, a_log, dt_bias, delta_norm_w, fox_forget_b, w_branch_a, w_branch_b, w_out,
              ln1_g, ln1_b, router_group_w, router_group_b, router_expert_w, router_expert_b,
              expert_w_gate, expert_w_up, expert_w_down, ln2_g, ln2_b):
    bp = x_prompt.shape[0]
    bs = x_sample.shape[0]
    n_pages = page_table.shape[1]
    past = n_pages * PAGE_SIZE
    dt_p = x_prompt.dtype
    hp, hs = x_prompt, x_sample
    st_p, st_s = [], []
    for l in range(DEPTH):
        mix_w = (w_in[l], conv_w[l], a_log[l], dt_bias[l], delta_norm_w[l], fox_forget_b[l],
                 w_branch_a[l], w_branch_b[l], w_out[l])
        moe_w = (router_group_w[l], router_group_b[l], router_expert_w[l], router_expert_b[l],
                 expert_w_gate[l], expert_w_up[l], expert_w_down[l])
        norm_w = (ln1_g[l], ln1_b[l], ln2_g[l], ln2_b[l])
        hp, sp = decoder_layer(
            hp, mix_w, moe_w, norm_w,
            jnp.zeros((bp, CONV_W - 1, C_CONV), dt_p),
            jnp.zeros((bp, H_A, DK_A, DV_A), jnp.float32),
            jnp.zeros((bp, 0, H_B, HD_B), dt_p), jnp.zeros((bp, 0, H_B, HD_B), dt_p),
            jnp.zeros((bp, 0, H_B), jnp.float32))
        st_p.append(sp)
        past_k = cache_k[page_table, l].reshape(bs, past, H_B, HD_B)
        past_v = cache_v[page_table, l].reshape(bs, past, H_B, HD_B)
        past_logf = cache_logf[page_table, l].reshape(bs, past, H_B)
        hs, ss = decoder_layer(hs, mix_w, moe_w, norm_w, state_conv[:, l], state_delta[:, l],
                               past_k, past_v, past_logf)
        st_s.append(ss)
    k_prompt = jnp.stack([s[0] for s in st_p], axis=1)
    v_prompt = jnp.stack([s[1] for s in st_p], axis=1)
    logf_prompt = jnp.stack([s[2] for s in st_p], axis=1)
    conv_prompt = jnp.stack([s[3] for s in st_p], axis=1)
    delta_prompt = jnp.stack([s[4] for s in st_p], axis=1)
    k_sample = jnp.stack([s[0] for s in st_s], axis=1)
    v_sample = jnp.stack([s[1] for s in st_s], axis=1)
    logf_sample = jnp.stack([s[2] for s in st_s], axis=1)
    conv_sample = jnp.stack([s[3] for s in st_s], axis=1)
    delta_sample = jnp.stack([s[4] for s in st_s], axis=1)
    return (hp, hs, k_prompt, v_prompt, logf_prompt, conv_prompt, delta_prompt,
            k_sample, v_sample, logf_sample, conv_sample, delta_sample)
```

```python
import functools

import jax
import jax.numpy as jnp
from jax import lax
from jax.experimental import pallas as pl
from jax.experimental.pallas import tpu as pltpu

F32 = jnp.float32
BF16 = jnp.bfloat16
HIGHEST = lax.Precision.HIGHEST

D_MODEL = 2048
N_HEADS = 8
HEAD_DIM = 128
W_BRANCH = N_HEADS * HEAD_DIM
C_CONV = 3 * W_BRANCH
CONV_W = 4
CHUNK = 64
N_GROUPS = 4
EXPERTS_PER_GROUP = 4
N_EXPERTS = 16
D_EXPERT = 512
ALPHA = 8.0 ** 0.25
LN_EPS = 1e-5
RMS_EPS = 1e-6
L2_EPS = 1e-6
LANES = 128
SUBLANES = 8

OFF_QKV = 0
OFF_Z = 3072
OFF_GA = 4096
OFF_GB = 6144
OFF_QB = 8192
OFF_KB = 9216
OFF_VB = 10240
OFF_SM = 11264
N_PACK = 11520
SM_BETA = 0
SM_DEC = 8
SM_F = 16
ROUTER_E0 = 4

VMEM_LIMIT = 56 * 1024 * 1024


def _cparams(sem):
    return pltpu.CompilerParams(dimension_semantics=sem, vmem_limit_bytes=VMEM_LIMIT)


def _sigmoid(x):
    return 1.0 / (1.0 + jnp.exp(-x))


def _softplus(x):
    return jnp.maximum(x, 0.0) + jnp.log1p(jnp.exp(-jnp.abs(x)))


def _dot(a, b):
    return jnp.dot(a, b, preferred_element_type=F32)


def _dot_hi(a, b):
    return jnp.dot(a, b, preferred_element_type=F32, precision=HIGHEST)


def _dot_nt(a, b, precision=None):
    return lax.dot_general(a, b, (((1,), (1,)), ((), ())), preferred_element_type=F32, precision=precision)


def _dot_tn(a, b, precision=None):
    return lax.dot_general(a, b, (((0,), (0,)), ((), ())), preferred_element_type=F32, precision=precision)


def _layer_norm(y, g, b):
    mu = jnp.mean(y, axis=-1, keepdims=True)
    yc = y - mu
    var = jnp.mean(yc * yc, axis=-1, keepdims=True)
    return yc * lax.rsqrt(var + LN_EPS) * g + b


def _proj_kernel(x_ref, w_ref, o_ref):
    o_ref[...] = _dot(x_ref[...].astype(BF16), w_ref[...])


def _proj(x, w_pack, layer, tm, tn):
    m, k = x.shape
    n = w_pack.shape[-1]
    return pl.pallas_call(
        _proj_kernel,
        grid=(n // tn, m // tm),
        in_specs=[pl.BlockSpec((tm, k), lambda j, i: (i, 0)),
                  pl.BlockSpec((None, k, tn), lambda j, i: (layer, 0, j))],
        out_specs=pl.BlockSpec((tm, tn), lambda j, i: (i, j)),
        out_shape=jax.ShapeDtypeStruct((m, n), F32),
        compiler_params=_cparams(("parallel", "parallel")),
        name="proj_in",
    )(x, w_pack)


def _delta_kernel(xq_ref, xk_ref, xv_ref, z_ref, sm_ref, wq_ref, wk_ref, wv_ref,
                  alog_ref, dtb_ref, nw_ref, o_ref, sout_ref,
                  xxq, xxk, xxv, s_scr, *, chunk, hp):
    c = chunk
    hg = pl.program_id(1)
    n = pl.program_id(2)

    @pl.when(n == 0)
    def _():
        xxq[0:SUBLANES, :] = jnp.zeros((SUBLANES, hp * HEAD_DIM), F32)
        xxk[0:SUBLANES, :] = jnp.zeros((SUBLANES, hp * HEAD_DIM), F32)
        xxv[0:SUBLANES, :] = jnp.zeros((SUBLANES, hp * HEAD_DIM), F32)
        s_scr[...] = jnp.zeros(s_scr.shape, F32)

    def conv(xx, x_ref, w_ref):
        xx[SUBLANES:SUBLANES + c, :] = x_ref[...]
        acc = w_ref[CONV_W - 1:CONV_W, :] * x_ref[...]
        for j in range(CONV_W - 1):
            off = SUBLANES - (CONV_W - 1) + j
            acc = acc + w_ref[j:j + 1, :] * xx[off:off + c, :]
        tail = xx[c:c + SUBLANES, :]
        xx[0:SUBLANES, :] = tail
        return acc * _sigmoid(acc)

    yq = conv(xxq, xq_ref, wq_ref)
    yk = conv(xxk, xk_ref, wk_ref)
    yv = conv(xxv, xv_ref, wv_ref)

    sm = sm_ref[...]
    lane = lax.broadcasted_iota(jnp.int32, (c, LANES), 1)
    beta_all = _sigmoid(sm)
    g_all = -jnp.exp(alog_ref[...]) * _softplus(sm + dtb_ref[...])
    row = lax.broadcasted_iota(jnp.int32, (c, c), 0)
    col = lax.broadcasted_iota(jnp.int32, (c, c), 1)
    tri_incl = row >= col
    tri_strict = row > col
    eye = jnp.where(row == col, 1.0, 0.0).astype(F32)
    cum_all = _dot_hi(jnp.where(tri_incl, 1.0, 0.0).astype(F32), g_all)
    ones_cl = jnp.ones((c, LANES), F32)

    for j in range(hp):
        hh = hg * hp + j
        sel_g = jnp.where(lane == SM_DEC + hh, cum_all, 0.0)
        gcol = jnp.sum(sel_g, axis=-1, keepdims=True)
        grow = _dot_nt(ones_cl, sel_g, HIGHEST)
        bcol = jnp.sum(jnp.where(lane == SM_BETA + hh, beta_all, 0.0), axis=-1, keepdims=True)
        decay = jnp.where(tri_incl, jnp.exp(jnp.where(tri_incl, gcol - grow, 0.0)), 0.0)

        hs = slice(j * HEAD_DIM, (j + 1) * HEAD_DIM)
        q = yq[:, hs]
        k = yk[:, hs]
        v = yv[:, hs]
        q = q * lax.rsqrt(jnp.sum(q * q, axis=-1, keepdims=True) + L2_EPS) * (HEAD_DIM ** -0.5)
        k = k * lax.rsqrt(jnp.sum(k * k, axis=-1, keepdims=True) + L2_EPS)
        qb = q.astype(BF16)
        kb = k.astype(BF16)

        kk = _dot_nt(kb, kb)
        a_neg = jnp.where(tri_strict, -(bcol * decay * kk), 0.0)
        t_inv = eye + a_neg
        xpow = a_neg
        for _ in range(max(0, (c - 1).bit_length() - 1)):
            xpow = _dot_hi(xpow, xpow)
            t_inv = t_inv + _dot_hi(t_inv, xpow)

        e_g = jnp.exp(gcol)
        rhs = jnp.concatenate([(bcol * e_g) * k, bcol * v], axis=1)
        sol = _dot_hi(t_inv, rhs)
        w_blk = sol[:, :HEAD_DIM]
        uv_blk = sol[:, HEAD_DIM:]
        qk = _dot_nt(qb, kb) * decay
        q_dec = q * e_g
        g_end = gcol[c - 1:c, :]
        k_dec = k * jnp.exp(g_end - gcol)

        s_old = s_scr[j]
        ws = _dot(jnp.concatenate([w_blk, q_dec], axis=0).astype(BF16), s_old.astype(BF16))
        u = uv_blk - ws[:c]
        ub = u.astype(BF16)
        o = ws[c:] + _dot(qk.astype(BF16), ub)
        s_scr[j] = jnp.exp(g_end) * s_old + _dot_tn(k_dec.astype(BF16), ub)

        zz = z_ref[:, hs]
        o = o * lax.rsqrt(jnp.mean(o * o, axis=-1, keepdims=True) + RMS_EPS) * nw_ref[...] * (zz * _sigmoid(zz))
        o_ref[:, hs] = o.astype(o_ref.dtype)

    @pl.when(n == pl.num_programs(2) - 1)
    def _():
        sout_ref[...] = s_scr[...]


def _delta_prompt(p, conv_w, a_log_row, dt_bias_row, norm_w_row, layer, batch, seq, hp=4):
    c = CHUNK if seq % CHUNK == 0 else seq
    nch = seq // c
    wb = hp * HEAD_DIM
    ngrp = N_HEADS // hp

    def xspec(off):
        return pl.BlockSpec((c, wb), lambda b, g, n: (b * nch + n, off // wb + g))

    def wspec(off):
        return pl.BlockSpec((None, CONV_W, wb), lambda b, g, n: (layer, 0, off // wb + g))

    rowspec = pl.BlockSpec((None, 1, LANES), lambda b, g, n: (layer, 0, 0))
    return pl.pallas_call(
        functools.partial(_delta_kernel, chunk=c, hp=hp),
        grid=(batch, ngrp, nch),
        in_specs=[xspec(OFF_QKV), xspec(OFF_QKV + W_BRANCH), xspec(OFF_QKV + 2 * W_BRANCH), xspec(OFF_Z),
                  pl.BlockSpec((c, LANES), lambda b, g, n: (b * nch + n, OFF_SM // LANES)),
                  wspec(0), wspec(W_BRANCH), wspec(2 * W_BRANCH),
                  rowspec, rowspec, rowspec],
        out_specs=[pl.BlockSpec((c, wb), lambda b, g, n: (b * nch + n, g)),
                   pl.BlockSpec((None, hp, HEAD_DIM, HEAD_DIM), lambda b, g, n: (b, g, 0, 0))],
        out_shape=[jax.ShapeDtypeStruct((batch * seq, W_BRANCH), BF16),
                   jax.ShapeDtypeStruct((batch, N_HEADS, HEAD_DIM, HEAD_DIM), F32)],
        scratch_shapes=[pltpu.VMEM((c + SUBLANES, wb), F32)] * 3 + [pltpu.VMEM((hp, HEAD_DIM, HEAD_DIM), F32)],
        compiler_params=_cparams(("parallel", "parallel", "arbitrary")),
        name="delta_prompt",
    )(p, p, p, p, p, conv_w, conv_w, conv_w, a_log_row, dt_bias_row, norm_w_row)


def _delta_step_kernel(x_ref, z_ref, sm_ref, cbuf_ref, w_ref, alog_ref, dtb_ref, nw_ref, s0_ref,
                       o_ref, sout_ref):
    x_new = x_ref[...]
    acc = w_ref[CONV_W - 1:CONV_W, :] * x_new
    for j in range(CONV_W - 1):
        acc = acc + w_ref[j:j + 1, :] * cbuf_ref[j:j + 1, :]
    y = acc * _sigmoid(acc)
    sm = sm_ref[...]
    lane = lax.broadcasted_iota(jnp.int32, (1, LANES), 1)
    beta_all = _sigmoid(sm)
    g_all = -jnp.exp(alog_ref[...]) * _softplus(sm + dtb_ref[...])
    rows8 = lax.broadcasted_iota(jnp.int32, (SUBLANES, HEAD_DIM), 0)

    def pad8(r):
        return jnp.where(rows8 == 0, r, 0.0)

    for h in range(N_HEADS):
        g = jnp.sum(jnp.where(lane == SM_DEC + h, g_all, 0.0), axis=-1, keepdims=True)
        beta = jnp.sum(jnp.where(lane == SM_BETA + h, beta_all, 0.0), axis=-1, keepdims=True)
        q = y[:, h * HEAD_DIM:(h + 1) * HEAD_DIM]
        k = y[:, W_BRANCH + h * HEAD_DIM:W_BRANCH + (h + 1) * HEAD_DIM]
        v = y[:, 2 * W_BRANCH + h * HEAD_DIM:2 * W_BRANCH + (h + 1) * HEAD_DIM]
        q = q * lax.rsqrt(jnp.sum(q * q, axis=-1, keepdims=True) + L2_EPS) * (HEAD_DIM ** -0.5)
        k = k * lax.rsqrt(jnp.sum(k * k, axis=-1, keepdims=True) + L2_EPS)
        e_g = jnp.exp(g)
        s_old = s0_ref[h]
        w_row = (beta * e_g) * k
        proj = _dot_hi(jnp.concatenate([pad8(w_row), pad8(q * e_g)], axis=0), s_old)
        u = beta * v - proj[0:1, :]
        qk = jnp.sum(q * k, axis=-1, keepdims=True)
        o = proj[SUBLANES:SUBLANES + 1, :] + qk * u
        s_new = e_g * s_old + _dot_tn(pad8(k), pad8(u), HIGHEST)
        sout_ref[h] = s_new
        zz = z_ref[:, h * HEAD_DIM:(h + 1) * HEAD_DIM]
        o = o * lax.rsqrt(jnp.mean(o * o, axis=-1, keepdims=True) + RMS_EPS) * nw_ref[...] * (zz * _sigmoid(zz))
        o_ref[:, h * HEAD_DIM:(h + 1) * HEAD_DIM] = o.astype(o_ref.dtype)


def _delta_sample(p3, state_conv, conv_w, a_log_row, dt_bias_row, norm_w_row, state_delta, layer):
    bs = p3.shape[0]
    rowspec = pl.BlockSpec((None, 1, LANES), lambda b: (layer, 0, 0))
    return pl.pallas_call(
        _delta_step_kernel,
        grid=(bs,),
        in_specs=[pl.BlockSpec((None, 1, C_CONV), lambda b: (b, 0, OFF_QKV // C_CONV)),
                  pl.BlockSpec((None, 1, W_BRANCH), lambda b: (b, 0, OFF_Z // W_BRANCH)),
                  pl.BlockSpec((None, 1, LANES), lambda b: (b, 0, OFF_SM // LANES)),
                  pl.BlockSpec((None, None, CONV_W - 1, C_CONV), lambda b: (b, layer, 0, 0)),
                  pl.BlockSpec((None, CONV_W, C_CONV), lambda b: (layer, 0, 0)),
                  rowspec, rowspec, rowspec,
                  pl.BlockSpec((None, None, N_HEADS, HEAD_DIM, HEAD_DIM), lambda b: (b, layer, 0, 0, 0))],
        out_specs=[pl.BlockSpec((None, 1, W_BRANCH), lambda b: (b, 0, 0)),
                   pl.BlockSpec((None, N_HEADS, HEAD_DIM, HEAD_DIM), lambda b: (b, 0, 0, 0))],
        out_shape=[jax.ShapeDtypeStruct((bs, 1, W_BRANCH), BF16),
                   jax.ShapeDtypeStruct((bs, N_HEADS, HEAD_DIM, HEAD_DIM), F32)],
        compiler_params=_cparams(("parallel",)),
        name="delta_sample",
    )(p3, p3, p3, state_conv, conv_w, a_log_row, dt_bias_row, norm_w_row, state_delta)


def _log_sigmoid(x):
    return jnp.minimum(x, 0.0) - jnp.log1p(jnp.exp(-jnp.abs(x)))


def _logf_kernel(sm_ref, fb_ref, logf_ref, ft_ref, carry):
    i = pl.program_id(1)

    @pl.when(i == 0)
    def _():
        carry[...] = jnp.zeros(carry.shape, F32)

    logf = _log_sigmoid(sm_ref[...] + fb_ref[...])
    logf_ref[...] = logf
    t = logf.shape[0]
    row = lax.broadcasted_iota(jnp.int32, (t, t), 0)
    col = lax.broadcasted_iota(jnp.int32, (t, t), 1)
    upper = jnp.where(row <= col, 1.0, 0.0).astype(F32)
    ft = _dot_hi(logf.T, upper) + carry[...]
    ft_ref[...] = ft
    carry[...] = ft[:, t - 1:t]


def _logf_prompt(p, fb_row, layer, batch, seq):
    t = LANES
    nblk = seq // t
    return pl.pallas_call(
        _logf_kernel,
        grid=(batch, nblk),
        in_specs=[pl.BlockSpec((t, LANES), lambda b, i: (b * nblk + i, OFF_SM // LANES)),
                  pl.BlockSpec((None, 1, LANES), lambda b, i: (layer, 0, 0))],
        out_specs=[pl.BlockSpec((t, LANES), lambda b, i: (b * nblk + i, 0)),
                   pl.BlockSpec((None, LANES, t), lambda b, i: (b, 0, i))],
        out_shape=[jax.ShapeDtypeStruct((batch * seq, LANES), F32),
                   jax.ShapeDtypeStruct((batch, LANES, seq), F32)],
        scratch_shapes=[pltpu.VMEM((LANES, 1), F32)],
        compiler_params=_cparams(("parallel", "arbitrary")),
        name="logf_prompt",
    )(p, fb_row)


def _fox_kernel(q_ref, k_ref, v_ref, f_ref, o_ref, kb_scr, vb_scr, m_scr, l_scr, acc_scr, *, tq, tk):
    i = pl.program_id(2)

    @pl.when(i == 0)
    def _():
        kb_scr[...] = k_ref[...].astype(BF16)
        vb_scr[...] = v_ref[...].astype(BF16)

    qb = (q_ref[...] * (HEAD_DIM ** -0.5)).astype(BF16)
    m_scr[...] = jnp.full(m_scr.shape, -jnp.inf, F32)
    l_scr[...] = jnp.zeros(l_scr.shape, F32)
    acc_scr[...] = jnp.zeros(acc_scr.shape, F32)

    def block(j, masked):
        start = pl.multiple_of(j * tk, tk)
        kb = kb_scr[pl.ds(start, tk), :]
        vb = vb_scr[pl.ds(start, tk), :]
        s = _dot_nt(qb, kb) - f_ref[:, pl.ds(start, tk)]
        if masked:
            row = lax.broadcasted_iota(jnp.int32, (tq, tk), 0)
            col = lax.broadcasted_iota(jnp.int32, (tq, tk), 1)
            s = jnp.where(row >= col, s, -jnp.inf)
        m_old = m_scr[...]
        m_new = jnp.maximum(m_old, jnp.max(s, axis=-1, keepdims=True))
        alpha = jnp.exp(m_old - m_new)
        p = jnp.exp(s - m_new)
        l_scr[...] = alpha * l_scr[...] + jnp.sum(p, axis=-1, keepdims=True)
        acc_scr[...] = alpha * acc_scr[...] + _dot(p.astype(BF16), vb)
        m_scr[...] = m_new

    def body(j, carry):
        block(j, False)
        return carry

    lax.fori_loop(0, i, body, 0)
    block(i, True)
    o_ref[...] = (acc_scr[...] / l_scr[...]).astype(o_ref.dtype)


def _fox_prompt(p, ft4, batch, seq, tq):
    nq = seq // tq
    return pl.pallas_call(
        functools.partial(_fox_kernel, tq=tq, tk=tq),
        grid=(batch, N_HEADS, nq),
        in_specs=[pl.BlockSpec((tq, HEAD_DIM), lambda b, h, i: (b * nq + i, OFF_QB // HEAD_DIM + h)),
                  pl.BlockSpec((seq, HEAD_DIM), lambda b, h, i: (b, OFF_KB // HEAD_DIM + h)),
                  pl.BlockSpec((seq, HEAD_DIM), lambda b, h, i: (b, OFF_VB // HEAD_DIM + h)),
                  pl.BlockSpec((None, None, 1, seq), lambda b, h, i: (b, SM_F + h, 0, 0))],
        out_specs=pl.BlockSpec((tq, HEAD_DIM), lambda b, h, i: (b * nq + i, h)),
        out_shape=jax.ShapeDtypeStruct((batch * seq, W_BRANCH), BF16),
        scratch_shapes=[pltpu.VMEM((seq, HEAD_DIM), BF16), pltpu.VMEM((seq, HEAD_DIM), BF16),
                        pltpu.VMEM((tq, 1), F32), pltpu.VMEM((tq, 1), F32), pltpu.VMEM((tq, HEAD_DIM), F32)],
        compiler_params=_cparams(("parallel", "parallel", "arbitrary")),
        name="fox_prompt",
    )(p, p, p, ft4)


def _fox_decode_kernel(pt_ref, q_ref, kn_ref, vn_ref, sm_ref, fb_ref, kc_ref, vc_ref, lfc_ref,
                       o_ref, logf_ref, qrows, base, m_scr, l_scr, acc_scr):
    del pt_ref
    j = pl.program_id(1)
    hrow = lax.broadcasted_iota(jnp.int32, (N_HEADS, W_BRANCH), 0)
    hcol = lax.broadcasted_iota(jnp.int32, (N_HEADS, W_BRANCH), 1) // HEAD_DIM

    @pl.when(j == 0)
    def _():
        qrows[...] = jnp.where(hrow == hcol, q_ref[...] * (HEAD_DIM ** -0.5), 0.0)
        base[...] = jnp.zeros(base.shape, F32)
        m_scr[...] = jnp.full(m_scr.shape, -jnp.inf, F32)
        l_scr[...] = jnp.zeros(l_scr.shape, F32)
        acc_scr[...] = jnp.zeros(acc_scr.shape, F32)

    t = kc_ref.shape[0]
    row = lax.broadcasted_iota(jnp.int32, (t, t), 0)
    col = lax.broadcasted_iota(jnp.int32, (t, t), 1)
    upper = jnp.where(row <= col, 1.0, 0.0).astype(F32)
    f_page = _dot_hi(lfc_ref[...], upper) + base[...]
    base[...] = f_page[:, t - 1:t]
    s = _dot_nt(qrows[...].astype(BF16), kc_ref[...].astype(BF16)) - f_page
    m_old = m_scr[...]
    m_new = jnp.maximum(m_old, jnp.max(s, axis=-1, keepdims=True))
    alpha = jnp.exp(m_old - m_new)
    p = jnp.exp(s - m_new)
    l_scr[...] = alpha * l_scr[...] + jnp.sum(p, axis=-1, keepdims=True)
    acc_scr[...] = alpha * acc_scr[...] + _dot(p.astype(BF16), vc_ref[...].astype(BF16))
    m_scr[...] = m_new

    @pl.when(j == pl.num_programs(1) - 1)
    def _():
        lane8 = lax.broadcasted_iota(jnp.int32, (N_HEADS, LANES), 1)
        sub8 = lax.broadcasted_iota(jnp.int32, (N_HEADS, LANES), 0)
        logf_row = _log_sigmoid(sm_ref[...] + fb_ref[...])
        logf_ref[...] = logf_row
        logf_new = jnp.sum(jnp.where(lane8 == SM_F + sub8, logf_row, 0.0), axis=-1, keepdims=True)
        s_new = jnp.sum(qrows[...] * kn_ref[...], axis=-1, keepdims=True) - (base[...] + logf_new)
        m_old = m_scr[...]
        m_new = jnp.maximum(m_old, s_new)
        alpha = jnp.exp(m_old - m_new)
        p_new = jnp.exp(s_new - m_new)
        l_fin = alpha * l_scr[...] + p_new
        acc = alpha * acc_scr[...] + p_new * vn_ref[...]
        out = jnp.sum(jnp.where(hrow == hcol, acc / l_fin, 0.0), axis=0, keepdims=True)
        o_ref[...] = out.astype(o_ref.dtype)


def _fox_sample(p3, fb_row, cache_k4, cache_v4, cache_logf_t, page_table, layer):
    bs, n_pages = page_table.shape
    page = cache_k4.shape[2]
    pt_flat = page_table.reshape(-1)

    def pspec(width, off):
        return pl.BlockSpec((None, 1, width), lambda b, j, pt: (b, 0, off // width))

    def cspec(last):
        return pl.BlockSpec((None, None, page, last), lambda b, j, pt: (pt[b * n_pages + j], layer, 0, 0))

    grid_spec = pltpu.PrefetchScalarGridSpec(
        num_scalar_prefetch=1,
        grid=(bs, n_pages),
        in_specs=[pspec(W_BRANCH, OFF_QB), pspec(W_BRANCH, OFF_KB), pspec(W_BRANCH, OFF_VB), pspec(LANES, OFF_SM),
                  pl.BlockSpec((None, 1, LANES), lambda b, j, pt: (layer, 0, 0)),
                  cspec(W_BRANCH), cspec(W_BRANCH),
                  pl.BlockSpec((None, None, N_HEADS, page), lambda b, j, pt: (pt[b * n_pages + j], layer, 0, 0))],
        out_specs=[pl.BlockSpec((None, 1, W_BRANCH), lambda b, j, pt: (b, 0, 0)),
                   pl.BlockSpec((None, 1, LANES), lambda b, j, pt: (b, 0, 0))],
        scratch_shapes=[pltpu.VMEM((N_HEADS, W_BRANCH), F32), pltpu.VMEM((N_HEADS, 1), F32),
                        pltpu.VMEM((N_HEADS, 1), F32), pltpu.VMEM((N_HEADS, 1), F32),
                        pltpu.VMEM((N_HEADS, W_BRANCH), F32)])
    return pl.pallas_call(
        _fox_decode_kernel,
        grid_spec=grid_spec,
        out_shape=[jax.ShapeDtypeStruct((bs, 1, W_BRANCH), BF16),
                   jax.ShapeDtypeStruct((bs, 1, LANES), F32)],
        compiler_params=_cparams(("parallel", "arbitrary")),
        name="fox_sample",
    )(pt_flat, p3, p3, p3, p3, fb_row, cache_k4, cache_v4, cache_logf_t)


def _merge_kernel(x_ref, oa_ref, ob_ref, ga_ref, gb_ref, wba_ref, wbb_ref, wo_ref, g_ref, b_ref, o_ref):
    br_a = _dot(oa_ref[...], wba_ref[...])
    br_b = _dot(ob_ref[...], wbb_ref[...])
    merged = _sigmoid(ga_ref[...]) * br_a + _sigmoid(gb_ref[...]) * br_b
    y = _dot(merged.astype(BF16), wo_ref[...])
    o_ref[...] = _layer_norm(ALPHA * x_ref[...] + y, g_ref[...], b_ref[...])


def _merge_out(x, oa, ob, p, wba, wbb, wo, ln_g, ln_b, layer, tm):
    m = x.shape[0]
    const3 = lambda i: (layer, 0, 0)
    return pl.pallas_call(
        _merge_kernel,
        grid=(m // tm,),
        in_specs=[pl.BlockSpec((tm, D_MODEL), lambda i: (i, 0)),
                  pl.BlockSpec((tm, W_BRANCH), lambda i: (i, 0)),
                  pl.BlockSpec((tm, W_BRANCH), lambda i: (i, 0)),
                  pl.BlockSpec((tm, D_MODEL), lambda i: (i, OFF_GA // D_MODEL)),
                  pl.BlockSpec((tm, D_MODEL), lambda i: (i, OFF_GB // D_MODEL)),
                  pl.BlockSpec((None, W_BRANCH, D_MODEL), const3),
                  pl.BlockSpec((None, W_BRANCH, D_MODEL), const3),
                  pl.BlockSpec((None, D_MODEL, D_MODEL), const3),
                  pl.BlockSpec((None, 1, D_MODEL), const3),
                  pl.BlockSpec((None, 1, D_MODEL), const3)],
        out_specs=pl.BlockSpec((tm, D_MODEL), lambda i: (i, 0)),
        out_shape=jax.ShapeDtypeStruct((m, D_MODEL), F32),
        compiler_params=_cparams(("parallel",)),
        name="merge_out_ln1",
    )(x, oa, ob, p, p, wba, wbb, wo, ln_g, ln_b)


def _router_kernel(x_ref, w_ref, b_ref, gate_ref):
    logits = _dot_hi(x_ref[...], w_ref[...]) + b_ref[...]
    tm = logits.shape[0]
    lane = lax.broadcasted_iota(jnp.int32, (tm, LANES), 1).astype(F32)
    neg = -jnp.inf
    big = float(LANES)
    is_grp = lane < N_GROUPS
    lg = jnp.where(is_grp, logits, neg)
    m_g = jnp.max(lg, axis=-1, keepdims=True)
    gi = jnp.min(jnp.where(lg == m_g, lane, big), axis=-1, keepdims=True)
    w_g = 1.0 / jnp.sum(jnp.where(is_grp, jnp.exp(lg - m_g), 0.0), axis=-1, keepdims=True)
    e_lane = lane - ROUTER_E0
    in_grp = (e_lane >= gi * EXPERTS_PER_GROUP) & (e_lane < (gi + 1) * EXPERTS_PER_GROUP)
    le = jnp.where(in_grp, logits, neg)
    v1 = jnp.max(le, axis=-1, keepdims=True)
    i1 = jnp.min(jnp.where(le == v1, lane, big), axis=-1, keepdims=True)
    le2 = jnp.where(lane == i1, neg, le)
    v2 = jnp.max(le2, axis=-1, keepdims=True)
    i2 = jnp.min(jnp.where(le2 == v2, lane, big), axis=-1, keepdims=True)
    e2 = jnp.exp(v2 - v1)
    w1 = w_g / (1.0 + e2)
    w2 = w_g * e2 / (1.0 + e2)
    gate_ref[...] = jnp.where(lane == i1, w1, jnp.where(lane == i2, w2, 0.0))


def _router(x, w_r, b_r, layer, tm):
    m = x.shape[0]
    return pl.pallas_call(
        _router_kernel,
        grid=(m // tm,),
        in_specs=[pl.BlockSpec((tm, D_MODEL), lambda i: (i, 0)),
                  pl.BlockSpec((None, D_MODEL, LANES), lambda i: (layer, 0, 0)),
                  pl.BlockSpec((None, 1, LANES), lambda i: (layer, 0, 0))],
        out_specs=pl.BlockSpec((tm, LANES), lambda i: (i, 0)),
        out_shape=jax.ShapeDtypeStruct((m, LANES), F32),
        compiler_params=_cparams(("parallel",)),
        name="router",
    )(x, w_r, b_r)


def _moe_kernel(x_ref, gate_ref, wg_ref, wu_ref, wd_ref, g_ref, b_ref, o_ref, xb_scr, acc_scr):
    e = pl.program_id(1)

    @pl.when(e == 0)
    def _():
        xb_scr[...] = x_ref[...].astype(BF16)
        acc_scr[...] = jnp.zeros(acc_scr.shape, F32)

    lane = lax.broadcasted_iota(jnp.int32, gate_ref.shape, 1)
    gcol = jnp.sum(jnp.where(lane == ROUTER_E0 + e, gate_ref[...], 0.0), axis=-1, keepdims=True)

    @pl.when(jnp.max(gcol) > 0.0)
    def _():
        xb = xb_scr[...]
        hg = _dot(xb, wg_ref[...])
        hu = _dot(xb, wu_ref[...])
        h = (hg * _sigmoid(hg)) * hu * gcol
        acc_scr[...] += _dot(h.astype(BF16), wd_ref[...])

    @pl.when(e == pl.num_programs(1) - 1)
    def _():
        o_ref[...] = _layer_norm(ALPHA * x_ref[...] + acc_scr[...], g_ref[...], b_ref[...])


def _moe(x, gate, wg, wu, wd, ln_g, ln_b, layer, tm):
    m = x.shape[0]
    return pl.pallas_call(
        _moe_kernel,
        grid=(m // tm, N_EXPERTS),
        in_specs=[pl.BlockSpec((tm, D_MODEL), lambda i, e: (i, 0)),
                  pl.BlockSpec((tm, LANES), lambda i, e: (i, 0)),
                  pl.BlockSpec((None, None, D_MODEL, D_EXPERT), lambda i, e: (layer, e, 0, 0)),
                  pl.BlockSpec((None, None, D_MODEL, D_EXPERT), lambda i, e: (layer, e, 0, 0)),
                  pl.BlockSpec((None, None, D_EXPERT, D_MODEL), lambda i, e: (layer, e, 0, 0)),
                  pl.BlockSpec((None, 1, D_MODEL), lambda i, e: (layer, 0, 0)),
                  pl.BlockSpec((None, 1, D_MODEL), lambda i, e: (layer, 0, 0))],
        out_specs=pl.BlockSpec((tm, D_MODEL), lambda i, e: (i, 0)),
        out_shape=jax.ShapeDtypeStruct((m, D_MODEL), F32),
        scratch_shapes=[pltpu.VMEM((tm, D_MODEL), BF16), pltpu.VMEM((tm, D_MODEL), F32)],
        compiler_params=_cparams(("parallel", "arbitrary")),
        name="moe_ln2",
    )(x, gate, wg, wu, wd, ln_g, ln_b)


def _pack_w_in(w_in):
    pad = jnp.zeros(w_in.shape[:-1] + (N_PACK - OFF_SM - 3 * N_HEADS,), w_in.dtype)
    s = 0
    qkv = w_in[..., s:s + C_CONV]; s += C_CONV
    z = w_in[..., s:s + W_BRANCH]; s += W_BRANCH
    beta = w_in[..., s:s + N_HEADS]; s += N_HEADS
    dec = w_in[..., s:s + N_HEADS]; s += N_HEADS
    qb = w_in[..., s:s + W_BRANCH]; s += W_BRANCH
    kb = w_in[..., s:s + W_BRANCH]; s += W_BRANCH
    vb = w_in[..., s:s + W_BRANCH]; s += W_BRANCH
    fb = w_in[..., s:s + N_HEADS]; s += N_HEADS
    ga = w_in[..., s:s + D_MODEL]; s += D_MODEL
    gb = w_in[..., s:s + D_MODEL]; s += D_MODEL
    return jnp.concatenate([qkv, z, ga, gb, qb, kb, vb, beta, dec, fb, pad], axis=-1).astype(BF16)


def _lane_row(v, off):
    depth, n = v.shape
    return jnp.zeros((depth, 1, LANES), F32).at[:, 0, off:off + n].set(v.astype(F32))


def _pick_tile(m, pref):
    return pref if m % pref == 0 else m


def kernel(x_prompt, x_sample, cache_k, cache_v, cache_logf, state_conv, state_delta, page_table, w_in, conv_w, a_log, dt_bias, delta_norm_w, fox_forget_b, w_branch_a, w_branch_b, w_out, ln1_g, ln1_b, router_group_w, router_group_b, router_expert_w, router_expert_b, expert_w_gate, expert_w_up, expert_w_down, ln2_g, ln2_b):
    bp, seq, _ = x_prompt.shape
    bs = x_sample.shape[0]
    depth = w_in.shape[0]
    n_pool, _, page, _, _ = cache_k.shape

    w_pack = _pack_w_in(w_in)
    wba = w_branch_a.astype(BF16)
    wbb = w_branch_b.astype(BF16)
    wo = w_out.astype(BF16)
    wg = expert_w_gate.astype(BF16)
    wu = expert_w_up.astype(BF16)
    wd = expert_w_down.astype(BF16)
    w_r = jnp.concatenate([router_group_w, router_expert_w,
                           jnp.zeros((depth, D_MODEL, LANES - N_GROUPS - N_EXPERTS), F32)], axis=-1)
    b_r = jnp.concatenate([router_group_b, router_expert_b,
                           jnp.zeros((depth, LANES - N_GROUPS - N_EXPERTS), F32)], axis=-1)[:, None, :]
    a_log_row = _lane_row(a_log, SM_DEC)
    dt_bias_row = _lane_row(dt_bias, SM_DEC)
    fb_row = _lane_row(fox_forget_b, SM_F)
    nw_row = delta_norm_w[:, None, :]
    ln1_g3, ln1_b3, ln2_g3, ln2_b3 = (t[:, None, :] for t in (ln1_g, ln1_b, ln2_g, ln2_b))
    cache_k4 = cache_k.reshape(n_pool, depth, page, W_BRANCH)
    cache_v4 = cache_v.reshape(n_pool, depth, page, W_BRANCH)
    cache_logf_t = jnp.swapaxes(cache_logf, 2, 3)

    mp = bp * seq
    hp = x_prompt.reshape(mp, D_MODEL)
    hs = x_sample.reshape(bs, D_MODEL)
    tm_p = _pick_tile(mp, 512)
    tq = _pick_tile(seq, 512)

    st_p = []
    st_s = []
    for l in range(depth):
        p = _proj(hp, w_pack, l, tm_p, 1280)
        oa, s_fin = _delta_prompt(p, conv_w, a_log_row, dt_bias_row, nw_row, l, bp, seq)
        logf, ft = _logf_prompt(p, fb_row, l, bp, seq)
        ob = _fox_prompt(p, ft.reshape(bp, LANES, 1, seq), bp, seq, tq)
        x1 = _merge_out(hp, oa, ob, p, wba, wbb, wo, ln1_g3, ln1_b3, l, _pick_tile(mp, 256))
        gate = _router(x1, w_r, b_r, l, tm_p)
        hp = _moe(x1, gate, wg, wu, wd, ln2_g3, ln2_b3, l, tm_p)
        p3 = p.reshape(bp, seq, N_PACK)
        st_p.append((p3[:, :, OFF_KB:OFF_KB + W_BRANCH].reshape(bp, seq, N_HEADS, HEAD_DIM),
                     p3[:, :, OFF_VB:OFF_VB + W_BRANCH].reshape(bp, seq, N_HEADS, HEAD_DIM),
                     logf.reshape(bp, seq, LANES)[:, :, SM_F:SM_F + N_HEADS],
                     p3[:, seq - (CONV_W - 1):, OFF_QKV:OFF_QKV + C_CONV],
                     s_fin))

        ps = _proj(hs, w_pack, l, bs, 1280)
        ps3 = ps.reshape(bs, 1, N_PACK)
        oa_s, s_new = _delta_sample(ps3, state_conv, conv_w, a_log_row, dt_bias_row, nw_row, state_delta, l)
        ob_s, logf_s = _fox_sample(ps3, fb_row, cache_k4, cache_v4, cache_logf_t, page_table, l)
        x1s = _merge_out(hs, oa_s.reshape(bs, W_BRANCH), ob_s.reshape(bs, W_BRANCH), ps, wba, wbb, wo,
                         ln1_g3, ln1_b3, l, bs)
        gate_s = _router(x1s, w_r, b_r, l, bs)
        hs = _moe(x1s, gate_s, wg, wu, wd, ln2_g3, ln2_b3, l, bs)
        new_buf = jnp.concatenate([state_conv[:, l, 1:], ps3[:, :, OFF_QKV:OFF_QKV + C_CONV]], axis=1)
        st_s.append((ps[:, OFF_KB:OFF_KB + W_BRANCH].reshape(bs, 1, N_HEADS, HEAD_DIM),
                     ps[:, OFF_VB:OFF_VB + W_BRANCH].reshape(bs, 1, N_HEADS, HEAD_DIM),
                     logf_s[:, :, SM_F:SM_F + N_HEADS],
                     new_buf,
                     s_new))

    outs_p = tuple(jnp.stack([s[i] for s in st_p], axis=1) for i in range(5))
    outs_s = tuple(jnp.stack([s[i] for s in st_s], axis=1) for i in range(5))
    return (hp.reshape(bp, seq, D_MODEL), hs.reshape(bs, 1, D_MODEL)) + outs_p + outs_s
```

```python
import functools

import jax
import jax.numpy as jnp
from jax import lax
from jax.experimental import pallas as pl
from jax.experimental.pallas import tpu as pltpu

F32 = jnp.float32
BF16 = jnp.bfloat16
HIGHEST = lax.Precision.HIGHEST

D_MODEL = 2048
N_HEADS = 8
HEAD_DIM = 128
W_BRANCH = N_HEADS * HEAD_DIM
C_CONV = 3 * W_BRANCH
CONV_W = 4
CHUNK = 64
N_GROUPS = 4
EXPERTS_PER_GROUP = 4
N_EXPERTS = 16
D_EXPERT = 512
ALPHA = 8.0 ** 0.25
LN_EPS = 1e-5
RMS_EPS = 1e-6
L2_EPS = 1e-6
LANES = 128
SUBLANES = 8

OFF_QKV = 0
OFF_Z = 3072
OFF_GA = 4096
OFF_GB = 6144
OFF_QB = 8192
N_MAIN = 9216
N_KVS = 2 * W_BRANCH + LANES
SM_BETA = 0
SM_DEC = 8
SM_F = 16
ROUTER_E0 = 4
PAGES_PER_STEP = 4
DELTA_GROUP = 4

VMEM_LIMIT = 56 * 1024 * 1024


def _cparams(sem):
    return pltpu.CompilerParams(dimension_semantics=sem, vmem_limit_bytes=VMEM_LIMIT)


def _sigmoid(x):
    return 1.0 / (1.0 + jnp.exp(-x))


def _softplus(x):
    return jnp.maximum(x, 0.0) + jnp.log1p(jnp.exp(-jnp.abs(x)))


def _log_sigmoid(x):
    return jnp.minimum(x, 0.0) - jnp.log1p(jnp.exp(-jnp.abs(x)))


def _dot(a, b):
    return jnp.dot(a, b, preferred_element_type=F32)


def _dot_hi(a, b):
    return jnp.dot(a, b, preferred_element_type=F32, precision=HIGHEST)


def _dot_nt(a, b, precision=None):
    return lax.dot_general(a, b, (((1,), (1,)), ((), ())), preferred_element_type=F32, precision=precision)


def _dot_tn(a, b, precision=None):
    return lax.dot_general(a, b, (((0,), (0,)), ((), ())), preferred_element_type=F32, precision=precision)


def _split2(a):
    hi = a.astype(BF16)
    return hi, (a - hi.astype(F32)).astype(BF16)


def _split3(a):
    hi = a.astype(BF16)
    r = a - hi.astype(F32)
    mid = r.astype(BF16)
    return hi, mid, (r - mid.astype(F32)).astype(BF16)


def _dot3(a, b):
    ah, al = _split2(a)
    bh, bl = _split2(b)
    return _dot(ah, bh) + (_dot(ah, bl) + _dot(al, bh))


def _dot_exact_rhs01(x, m01):
    hi, mid, lo = _split3(x)
    return _dot(hi, m01) + (_dot(mid, m01) + _dot(lo, m01))


def _layer_norm(y, g, b):
    mu = jnp.mean(y, axis=-1, keepdims=True)
    yc = y - mu
    var = jnp.mean(yc * yc, axis=-1, keepdims=True)
    return yc * lax.rsqrt(var + LN_EPS) * g + b


def _proj_kernel(x_ref, w_ref, o_ref):
    o_ref[...] = _dot(x_ref[...].astype(BF16), w_ref[...])


def _proj(x, w_main, layer, tm, tn):
    m, k = x.shape
    n = w_main.shape[-1]
    return pl.pallas_call(
        _proj_kernel,
        grid=(n // tn, m // tm),
        in_specs=[pl.BlockSpec((tm, k), lambda j, i: (i, 0)),
                  pl.BlockSpec((None, k, tn), lambda j, i: (layer, 0, j))],
        out_specs=pl.BlockSpec((tm, tn), lambda j, i: (i, j)),
        out_shape=jax.ShapeDtypeStruct((m, n), F32),
        compiler_params=_cparams(("parallel", "parallel")),
        name="proj_in",
    )(x, w_main)


def _proj_kvs_kernel(x_ref, w_ref, *refs):
    k_ref, v_ref, sm_ref = refs[-3:]
    r = _dot(x_ref[...].astype(BF16), w_ref[...])
    k_ref[...] = r[:, :W_BRANCH]
    v_ref[...] = r[:, W_BRANCH:2 * W_BRANCH]
    sm_ref[...] = r[:, 2 * W_BRANCH:]


def _proj_kvs(x, w_kvs, layer, tm, batch, seq, slots, slot, k_all=None, v_all=None):
    m, k = x.shape
    nt = seq // tm
    kv_spec = pl.BlockSpec((None, None, tm, W_BRANCH), lambda i: (i // nt, slot, i % nt, 0))
    kv_shape = jax.ShapeDtypeStruct((batch, slots, seq, W_BRANCH), F32)
    in_specs = [pl.BlockSpec((tm, k), lambda i: (i, 0)),
                pl.BlockSpec((None, k, N_KVS), lambda i: (layer, 0, 0))]
    args = [x, w_kvs]
    aliases = {}
    if k_all is not None:
        in_specs += [pl.BlockSpec(memory_space=pl.ANY), pl.BlockSpec(memory_space=pl.ANY)]
        args += [k_all, v_all]
        aliases = {2: 0, 3: 1}
    return pl.pallas_call(
        _proj_kvs_kernel,
        grid=(m // tm,),
        in_specs=in_specs,
        out_specs=[kv_spec, kv_spec, pl.BlockSpec((tm, LANES), lambda i: (i, 0))],
        out_shape=[kv_shape, kv_shape, jax.ShapeDtypeStruct((m, LANES), F32)],
        input_output_aliases=aliases,
        compiler_params=_cparams(("parallel",)),
        name="proj_kvs",
    )(*args)


def _delta_kernel(x_ref, z_ref, sm_ref, w_ref, alog_ref, dtb_ref, nw_ref, o_ref, sout_ref, xx, s_scr, *, chunk):
    c = chunk
    n = pl.program_id(1)

    @pl.when(n == 0)
    def _():
        xx[0:SUBLANES, :] = jnp.zeros((SUBLANES, C_CONV), F32)
        s_scr[...] = jnp.zeros(s_scr.shape, F32)

    xx[SUBLANES:SUBLANES + c, :] = x_ref[...]
    acc = w_ref[CONV_W - 1:CONV_W, :] * x_ref[...]
    for j in range(CONV_W - 1):
        off = SUBLANES - (CONV_W - 1) + j
        acc = acc + w_ref[j:j + 1, :] * xx[off:off + c, :]
    tail = xx[c:c + SUBLANES, :]
    xx[0:SUBLANES, :] = tail
    y = acc * _sigmoid(acc)

    sm = sm_ref[...]
    beta_all = _sigmoid(sm)
    g_all = -jnp.exp(alog_ref[...]) * _softplus(sm + dtb_ref[...])
    r1 = lax.broadcasted_iota(jnp.int32, (c, c), 0)
    c1 = lax.broadcasted_iota(jnp.int32, (c, c), 1)
    lower01 = jnp.where(r1 >= c1, 1.0, 0.0).astype(BF16)
    upper01 = jnp.where(r1 <= c1, 1.0, 0.0).astype(BF16)
    g_hi, g_mid, g_lo = _split3(g_all)
    cum_col = _dot(jnp.concatenate([lower01] * 3, axis=1), jnp.concatenate([g_hi, g_mid, g_lo], axis=0))
    cum_row = _dot_tn(jnp.concatenate([g_hi, g_mid, g_lo], axis=0), jnp.concatenate([upper01] * 3, axis=0))

    gw = DELTA_GROUP * c
    sw = DELTA_GROUP * HEAD_DIM
    row = lax.broadcasted_iota(jnp.int32, (gw, gw), 0)
    col = lax.broadcasted_iota(jnp.int32, (gw, gw), 1)
    same = (row // c) == (col // c)
    tri_incl = same & (row >= col)
    tri_strict = same & (row > col)
    own_block = (lax.broadcasted_iota(jnp.int32, (gw, sw), 0) // c) == (lax.broadcasted_iota(jnp.int32, (gw, sw), 1) // HEAD_DIM)

    def stack(ref_or_val, base):
        return jnp.concatenate([ref_or_val[:, base + h * HEAD_DIM:base + (h + 1) * HEAD_DIM] for h in heads], axis=0)

    for g in range(N_HEADS // DELTA_GROUP):
        heads = range(g * DELTA_GROUP, (g + 1) * DELTA_GROUP)
        gcol = jnp.concatenate([jnp.broadcast_to(cum_col[:, SM_DEC + h:SM_DEC + h + 1], (c, gw)) for h in heads], axis=0)
        grow = jnp.broadcast_to(jnp.concatenate([cum_row[SM_DEC + h:SM_DEC + h + 1, :] for h in heads], axis=1), (gw, gw))
        bcol = jnp.concatenate([jnp.broadcast_to(beta_all[:, SM_BETA + h:SM_BETA + h + 1], (c, gw)) for h in heads], axis=0)
        gend = jnp.concatenate([jnp.broadcast_to(cum_col[c - 1:c, SM_DEC + h:SM_DEC + h + 1], (c, HEAD_DIM)) for h in heads], axis=0)
        gend_row = jnp.concatenate([jnp.broadcast_to(cum_col[c - 1:c, SM_DEC + h:SM_DEC + h + 1], (1, HEAD_DIM)) for h in heads], axis=1)
        decay = jnp.where(tri_incl, jnp.exp(jnp.where(tri_incl, gcol - grow, 0.0)), 0.0)
        gcol1 = gcol[:, :HEAD_DIM]
        bcol1 = bcol[:, :HEAD_DIM]

        q = stack(y, 0)
        k = stack(y, W_BRANCH)
        v = stack(y, 2 * W_BRANCH)
        q = q * lax.rsqrt(jnp.sum(q * q, axis=-1, keepdims=True) + L2_EPS) * (HEAD_DIM ** -0.5)
        k = k * lax.rsqrt(jnp.sum(k * k, axis=-1, keepdims=True) + L2_EPS)
        qb = q.astype(BF16)
        kb = k.astype(BF16)

        qkk = _dot_nt(jnp.concatenate([qb, kb], axis=0), kb)
        qk = qkk[:gw] * decay
        xf = jnp.where(tri_strict, -(bcol * decay * qkk[gw:]), 0.0)
        xb = xf.astype(BF16)
        e_g = jnp.exp(gcol1)
        sol = jnp.concatenate([(bcol1 * e_g) * k, bcol1 * v], axis=1)
        p = 1
        while p < c:
            if p == 1 and 2 * p < c:
                s_hi, s_lo = _split2(sol)
                x_hi, x_lo = _split2(xf)
                hi_row = jnp.concatenate([s_hi, x_hi], axis=1)
                prod = _dot(jnp.concatenate([x_hi, x_hi, x_lo], axis=1),
                            jnp.concatenate([hi_row, jnp.concatenate([s_lo, x_lo], axis=1), hi_row], axis=0))
                sol = sol + prod[:, :2 * HEAD_DIM]
                xb = prod[:, 2 * HEAD_DIM:].astype(BF16)
            elif 2 * p < c:
                prod = _dot(xb, jnp.concatenate([sol.astype(BF16), xb], axis=1))
                sol = sol + prod[:, :2 * HEAD_DIM]
                xb = prod[:, 2 * HEAD_DIM:].astype(BF16)
            else:
                sol = sol + _dot(xb, sol.astype(BF16))
            p *= 2
        w_blk = sol[:, :HEAD_DIM]
        uv_blk = sol[:, HEAD_DIM:]
        q_dec = q * e_g
        k_dec = k * jnp.exp(gend - gcol1)

        s_old = s_scr[g]
        res = _dot(jnp.concatenate([w_blk, q_dec], axis=0).astype(BF16), s_old.astype(BF16))
        ws = jnp.concatenate([res[i * c:(i + 1) * c, i * HEAD_DIM:(i + 1) * HEAD_DIM] for i in range(DELTA_GROUP)], axis=0)
        qs = jnp.concatenate([res[gw + i * c:gw + (i + 1) * c, i * HEAD_DIM:(i + 1) * HEAD_DIM]
                              for i in range(DELTA_GROUP)], axis=0)
        u = uv_blk - ws
        o = qs + _dot(qk.astype(BF16), u.astype(BF16))
        u_blocks = jnp.where(own_block, jnp.concatenate([u] * DELTA_GROUP, axis=1), 0.0).astype(BF16)
        s_scr[g] = jnp.exp(gend_row) * s_old + _dot_tn(k_dec.astype(BF16), u_blocks)

        zz = stack(z_ref, 0)
        o = o * lax.rsqrt(jnp.mean(o * o, axis=-1, keepdims=True) + RMS_EPS) * nw_ref[...] * (zz * _sigmoid(zz))
        for i, h in enumerate(heads):
            o_ref[:, h * HEAD_DIM:(h + 1) * HEAD_DIM] = o[i * c:(i + 1) * c, :].astype(o_ref.dtype)

    @pl.when(n == pl.num_programs(1) - 1)
    def _():
        for h in range(N_HEADS):
            i = h % DELTA_GROUP
            sout_ref[h] = s_scr[h // DELTA_GROUP, :, i * HEAD_DIM:(i + 1) * HEAD_DIM]


def _delta_prompt(p, sm, conv_w, a_log_row, dt_bias_row, norm_w_row, layer, batch, seq):
    c = CHUNK if seq % CHUNK == 0 else seq
    nch = seq // c
    rowspec = pl.BlockSpec((None, 1, LANES), lambda b, n: (layer, 0, 0))
    return pl.pallas_call(
        functools.partial(_delta_kernel, chunk=c),
        grid=(batch, nch),
        in_specs=[pl.BlockSpec((c, C_CONV), lambda b, n: (b * nch + n, OFF_QKV // C_CONV)),
                  pl.BlockSpec((c, W_BRANCH), lambda b, n: (b * nch + n, OFF_Z // W_BRANCH)),
                  pl.BlockSpec((c, LANES), lambda b, n: (b * nch + n, 0)),
                  pl.BlockSpec((None, CONV_W, C_CONV), lambda b, n: (layer, 0, 0)),
                  rowspec, rowspec, rowspec],
        out_specs=[pl.BlockSpec((c, W_BRANCH), lambda b, n: (b * nch + n, 0)),
                   pl.BlockSpec((None, N_HEADS, HEAD_DIM, HEAD_DIM), lambda b, n: (b, 0, 0, 0))],
        out_shape=[jax.ShapeDtypeStruct((batch * seq, W_BRANCH), BF16),
                   jax.ShapeDtypeStruct((batch, N_HEADS, HEAD_DIM, HEAD_DIM), F32)],
        scratch_shapes=[pltpu.VMEM((c + SUBLANES, C_CONV), F32),
                        pltpu.VMEM((N_HEADS // DELTA_GROUP, HEAD_DIM, DELTA_GROUP * HEAD_DIM), F32)],
        compiler_params=_cparams(("parallel", "arbitrary")),
        name="delta_prompt",
    )(p, p, sm, conv_w, a_log_row, dt_bias_row, norm_w_row)


def _delta_step_kernel(x_ref, z_ref, sm_ref, cbuf_ref, w_ref, alog_ref, dtb_ref, nw_ref, s0_ref,
                       o_ref, sout_ref):
    x_new = x_ref[...]
    acc = w_ref[CONV_W - 1:CONV_W, :] * x_new
    for j in range(CONV_W - 1):
        acc = acc + w_ref[j:j + 1, :] * cbuf_ref[j:j + 1, :]
    y = acc * _sigmoid(acc)
    sm = sm_ref[...]
    lane = lax.broadcasted_iota(jnp.int32, (1, LANES), 1)
    beta_all = _sigmoid(sm)
    g_all = -jnp.exp(alog_ref[...]) * _softplus(sm + dtb_ref[...])
    rows8 = lax.broadcasted_iota(jnp.int32, (SUBLANES, HEAD_DIM), 0)

    def pad8(r):
        return jnp.where(rows8 == 0, r, 0.0)

    for h in range(N_HEADS):
        g = jnp.sum(jnp.where(lane == SM_DEC + h, g_all, 0.0), axis=-1, keepdims=True)
        beta = jnp.sum(jnp.where(lane == SM_BETA + h, beta_all, 0.0), axis=-1, keepdims=True)
        q = y[:, h * HEAD_DIM:(h + 1) * HEAD_DIM]
        k = y[:, W_BRANCH + h * HEAD_DIM:W_BRANCH + (h + 1) * HEAD_DIM]
        v = y[:, 2 * W_BRANCH + h * HEAD_DIM:2 * W_BRANCH + (h + 1) * HEAD_DIM]
        q = q * lax.rsqrt(jnp.sum(q * q, axis=-1, keepdims=True) + L2_EPS) * (HEAD_DIM ** -0.5)
        k = k * lax.rsqrt(jnp.sum(k * k, axis=-1, keepdims=True) + L2_EPS)
        e_g = jnp.exp(g)
        s_old = s0_ref[h]
        w_row = (beta * e_g) * k
        proj = _dot_hi(jnp.concatenate([pad8(w_row), pad8(q * e_g)], axis=0), s_old)
        u = beta * v - proj[0:1, :]
        qk = jnp.sum(q * k, axis=-1, keepdims=True)
        o = proj[SUBLANES:SUBLANES + 1, :] + qk * u
        s_new = e_g * s_old + _dot_tn(pad8(k), pad8(u), HIGHEST)
        sout_ref[h] = s_new
        zz = z_ref[:, h * HEAD_DIM:(h + 1) * HEAD_DIM]
        o = o * lax.rsqrt(jnp.mean(o * o, axis=-1, keepdims=True) + RMS_EPS) * nw_ref[...] * (zz * _sigmoid(zz))
        o_ref[:, h * HEAD_DIM:(h + 1) * HEAD_DIM] = o.astype(o_ref.dtype)


def _delta_sample(p3, sm3, state_conv, conv_w, a_log_row, dt_bias_row, norm_w_row, state_delta, layer):
    bs = p3.shape[0]
    rowspec = pl.BlockSpec((None, 1, LANES), lambda b: (layer, 0, 0))
    return pl.pallas_call(
        _delta_step_kernel,
        grid=(bs,),
        in_specs=[pl.BlockSpec((None, 1, C_CONV), lambda b: (b, 0, OFF_QKV // C_CONV)),
                  pl.BlockSpec((None, 1, W_BRANCH), lambda b: (b, 0, OFF_Z // W_BRANCH)),
                  pl.BlockSpec((None, 1, LANES), lambda b: (b, 0, 0)),
                  pl.BlockSpec((None, None, CONV_W - 1, C_CONV), lambda b: (b, layer, 0, 0)),
                  pl.BlockSpec((None, CONV_W, C_CONV), lambda b: (layer, 0, 0)),
                  rowspec, rowspec, rowspec,
                  pl.BlockSpec((None, None, N_HEADS, HEAD_DIM, HEAD_DIM), lambda b: (b, layer, 0, 0, 0))],
        out_specs=[pl.BlockSpec((None, 1, W_BRANCH), lambda b: (b, 0, 0)),
                   pl.BlockSpec((None, N_HEADS, HEAD_DIM, HEAD_DIM), lambda b: (b, 0, 0, 0))],
        out_shape=[jax.ShapeDtypeStruct((bs, 1, W_BRANCH), BF16),
                   jax.ShapeDtypeStruct((bs, N_HEADS, HEAD_DIM, HEAD_DIM), F32)],
        compiler_params=_cparams(("parallel",)),
        name="delta_sample",
    )(p3, p3, sm3, state_conv, conv_w, a_log_row, dt_bias_row, norm_w_row, state_delta)


def _logf_kernel(sm_ref, fb_ref, logf_ref, ft_ref, carry):
    i = pl.program_id(1)

    @pl.when(i == 0)
    def _():
        carry[...] = jnp.zeros(carry.shape, F32)

    logf = _log_sigmoid(sm_ref[...] + fb_ref[...])
    logf_ref[...] = logf
    t = logf.shape[0]
    row = lax.broadcasted_iota(jnp.int32, (t, t), 0)
    col = lax.broadcasted_iota(jnp.int32, (t, t), 1)
    upper = jnp.where(row <= col, 1.0, 0.0).astype(F32)
    ft = _dot_hi(logf.T, upper) + carry[...]
    ft_ref[...] = ft
    carry[...] = ft[:, t - 1:t]


def _logf_prompt(sm, fb_row, layer, batch, seq):
    t = LANES
    nblk = seq // t
    return pl.pallas_call(
        _logf_kernel,
        grid=(batch, nblk),
        in_specs=[pl.BlockSpec((t, LANES), lambda b, i: (b * nblk + i, 0)),
                  pl.BlockSpec((None, 1, LANES), lambda b, i: (layer, 0, 0))],
        out_specs=[pl.BlockSpec((t, LANES), lambda b, i: (b * nblk + i, 0)),
                   pl.BlockSpec((None, LANES, t), lambda b, i: (b, 0, i))],
        out_shape=[jax.ShapeDtypeStruct((batch * seq, LANES), F32),
                   jax.ShapeDtypeStruct((batch, LANES, seq), F32)],
        scratch_shapes=[pltpu.VMEM((LANES, 1), F32)],
        compiler_params=_cparams(("parallel", "arbitrary")),
        name="logf_prompt",
    )(sm, fb_row)


def _fox_kernel(q_ref, k_ref, v_ref, f_ref, o_ref, kb_scr, vb_scr, m_scr, l_scr, acc_scr, *, tq, tk):
    i = pl.program_id(2)

    @pl.when(i == 0)
    def _():
        kb_scr[...] = k_ref[...].astype(BF16)
        vb_scr[...] = v_ref[...].astype(BF16)

    qb = (q_ref[...] * (HEAD_DIM ** -0.5)).astype(BF16)
    m_scr[...] = jnp.full(m_scr.shape, -jnp.inf, F32)
    l_scr[...] = jnp.zeros(l_scr.shape, F32)
    acc_scr[...] = jnp.zeros(acc_scr.shape, F32)

    def block(j, masked):
        start = pl.multiple_of(j * tk, tk)
        kb = kb_scr[pl.ds(start, tk), :]
        vb = vb_scr[pl.ds(start, tk), :]
        s = _dot_nt(qb, kb) - f_ref[:, pl.ds(start, tk)]
        if masked:
            row = lax.broadcasted_iota(jnp.int32, (tq, tk), 0)
            col = lax.broadcasted_iota(jnp.int32, (tq, tk), 1)
            s = jnp.where(row >= col, s, -jnp.inf)
        m_old = m_scr[...]
        m_new = jnp.maximum(m_old, jnp.max(s, axis=-1, keepdims=True))
        alpha = jnp.exp(m_old - m_new)
        p = jnp.exp(s - m_new)
        l_scr[...] = alpha * l_scr[...] + jnp.sum(p, axis=-1, keepdims=True)
        acc_scr[...] = alpha * acc_scr[...] + _dot(p.astype(BF16), vb)
        m_scr[...] = m_new

    def body(j, carry):
        block(j, False)
        return carry

    lax.fori_loop(0, i, body, 0)
    block(i, True)
    o_ref[...] = (acc_scr[...] / l_scr[...]).astype(o_ref.dtype)


def _fox_prompt(p, k_all, v_all, ft4, layer, batch, seq, tq):
    nq = seq // tq
    kv_spec = pl.BlockSpec((None, None, seq, HEAD_DIM), lambda b, h, i: (b, layer, 0, h))
    return pl.pallas_call(
        functools.partial(_fox_kernel, tq=tq, tk=tq),
        grid=(batch, N_HEADS, nq),
        in_specs=[pl.BlockSpec((tq, HEAD_DIM), lambda b, h, i: (b * nq + i, OFF_QB // HEAD_DIM + h)),
                  kv_spec, kv_spec,
                  pl.BlockSpec((None, None, 1, seq), lambda b, h, i: (b, SM_F + h, 0, 0))],
        out_specs=pl.BlockSpec((tq, HEAD_DIM), lambda b, h, i: (b * nq + i, h)),
        out_shape=jax.ShapeDtypeStruct((batch * seq, W_BRANCH), BF16),
        scratch_shapes=[pltpu.VMEM((seq, HEAD_DIM), BF16), pltpu.VMEM((seq, HEAD_DIM), BF16),
                        pltpu.VMEM((tq, 1), F32), pltpu.VMEM((tq, 1), F32), pltpu.VMEM((tq, HEAD_DIM), F32)],
        compiler_params=_cparams(("parallel", "parallel", "arbitrary")),
        name="fox_prompt",
    )(p, k_all, v_all, ft4)


def _fox_decode_kernel(pt_ref, q_ref, kn_ref, vn_ref, sm_ref, fb_ref, *refs, up):
    del pt_ref
    k_refs = refs[0:up]
    v_refs = refs[up:2 * up]
    lf_refs = refs[2 * up:3 * up]
    o_ref, logf_ref, kb_scr, vb_scr, base, m_scr, l_scr, acc_scr = refs[3 * up:]
    j = pl.program_id(1)
    rows = k_refs[0].shape[0]
    nsub = rows // LANES
    nrow = up * nsub

    @pl.when(j == 0)
    def _():
        base[...] = jnp.zeros(base.shape, F32)
        m_scr[...] = jnp.full(m_scr.shape, -jnp.inf, F32)
        l_scr[...] = jnp.zeros(l_scr.shape, F32)
        acc_scr[...] = jnp.zeros(acc_scr.shape, F32)

    for u in range(up):
        kb_scr[u * rows:(u + 1) * rows, :] = k_refs[u][...].astype(BF16)
        vb_scr[u * rows:(u + 1) * rows, :] = v_refs[u][...].astype(BF16)

    ci = lax.broadcasted_iota(jnp.int32, (LANES, LANES), 0)
    cj = lax.broadcasted_iota(jnp.int32, (LANES, LANES), 1)
    same_head = (ci % N_HEADS) == (cj % N_HEADS)
    cum01 = jnp.where(same_head & (ci // N_HEADS <= cj // N_HEADS), 1.0, 0.0).astype(BF16)
    tot01 = jnp.where(same_head, 1.0, 0.0).astype(BF16)
    both01 = jnp.concatenate([cum01, tot01], axis=1)
    ri = lax.broadcasted_iota(jnp.int32, (nrow, nrow), 0)
    rj = lax.broadcasted_iota(jnp.int32, (nrow, nrow), 1)
    below01 = jnp.where(ri > rj, 1.0, 0.0).astype(BF16)
    lf = jnp.concatenate([lf_refs[u][...] for u in range(up)], axis=0)
    lf_hi, lf_mid, lf_lo = _split3(lf)
    r = _dot(jnp.concatenate([lf_hi, lf_mid, lf_lo], axis=1), jnp.concatenate([both01] * 3, axis=0))
    within = r[:, :LANES]
    totals = r[:, LANES:]
    t_hi, t_mid, t_lo = _split3(totals)
    before = _dot(jnp.concatenate([below01] * 3, axis=1), jnp.concatenate([t_hi, t_mid, t_lo], axis=0))
    f_rows = within + before + base[...]
    base[...] = base[...] + jnp.sum(totals, axis=0, keepdims=True)

    qb = (q_ref[...] * (HEAD_DIM ** -0.5)).astype(BF16)
    s_all = _dot_nt(qb, kb_scr[...])
    own_head = (lax.broadcasted_iota(jnp.int32, (N_HEADS, LANES), 1) % N_HEADS
                == lax.broadcasted_iota(jnp.int32, (N_HEADS, LANES), 0))
    s = jnp.concatenate([jnp.where(own_head, s_all[:, a * LANES:(a + 1) * LANES] - f_rows[a:a + 1, :], -jnp.inf)
                         for a in range(nrow)], axis=1)
    m_old = m_scr[...]
    m_new = jnp.maximum(m_old, jnp.max(s, axis=-1, keepdims=True))
    alpha = jnp.exp(m_old - m_new)
    p = jnp.exp(s - m_new)
    l_scr[...] = alpha * l_scr[...] + jnp.sum(p, axis=-1, keepdims=True)
    acc_scr[...] = alpha * acc_scr[...] + _dot(p.astype(BF16), vb_scr[...])
    m_scr[...] = m_new

    @pl.when(j == pl.num_programs(1) - 1)
    def _():
        lane8 = lax.broadcasted_iota(jnp.int32, (N_HEADS, LANES), 1)
        sub8 = lax.broadcasted_iota(jnp.int32, (N_HEADS, LANES), 0)
        logf_row = _log_sigmoid(sm_ref[...] + fb_ref[...])
        logf_ref[...] = logf_row
        logf_new = jnp.sum(jnp.where(lane8 == SM_F + sub8, logf_row, 0.0), axis=-1, keepdims=True)
        f_past = jnp.sum(jnp.where(lane8 == sub8, base[...], 0.0), axis=-1, keepdims=True)
        s_new = jnp.sum(q_ref[...] * (HEAD_DIM ** -0.5) * kn_ref[...], axis=-1, keepdims=True) - (f_past + logf_new)
        m_prev = m_scr[...]
        m_fin = jnp.maximum(m_prev, s_new)
        a_fin = jnp.exp(m_prev - m_fin)
        p_new = jnp.exp(s_new - m_fin)
        l_fin = a_fin * l_scr[...] + p_new
        o_ref[...] = ((a_fin * acc_scr[...] + p_new * vn_ref[...]) / l_fin).astype(o_ref.dtype)


def _fox_sample(q3, kn3, vn3, sm3, fb_row, cache_k4, cache_v4, cache_lf4, page_table, layer):
    bs, n_pages = page_table.shape
    rows = cache_k4.shape[2]
    nsub = cache_lf4.shape[2]
    up = PAGES_PER_STEP if n_pages % PAGES_PER_STEP == 0 else 1
    pt_flat = page_table.reshape(-1)
    hspec = pl.BlockSpec((None, N_HEADS, HEAD_DIM), lambda b, j, pt: (b, 0, 0))

    def cspec(u, second, last):
        return pl.BlockSpec((None, None, second, last),
                            lambda b, j, pt: (pt[b * n_pages + j * up + u], layer, 0, 0))

    grid_spec = pltpu.PrefetchScalarGridSpec(
        num_scalar_prefetch=1,
        grid=(bs, n_pages // up),
        in_specs=[hspec, hspec, hspec,
                  pl.BlockSpec((None, 1, LANES), lambda b, j, pt: (b, 0, 0)),
                  pl.BlockSpec((None, 1, LANES), lambda b, j, pt: (layer, 0, 0))]
                 + [cspec(u, rows, HEAD_DIM) for u in range(up)]
                 + [cspec(u, rows, HEAD_DIM) for u in range(up)]
                 + [cspec(u, nsub, LANES) for u in range(up)],
        out_specs=[hspec, pl.BlockSpec((None, 1, LANES), lambda b, j, pt: (b, 0, 0))],
        scratch_shapes=[pltpu.VMEM((up * rows, HEAD_DIM), BF16), pltpu.VMEM((up * rows, HEAD_DIM), BF16),
                        pltpu.VMEM((1, LANES), F32), pltpu.VMEM((N_HEADS, 1), F32),
                        pltpu.VMEM((N_HEADS, 1), F32), pltpu.VMEM((N_HEADS, HEAD_DIM), F32)])
    return pl.pallas_call(
        functools.partial(_fox_decode_kernel, up=up),
        grid_spec=grid_spec,
        out_shape=[jax.ShapeDtypeStruct((bs, N_HEADS, HEAD_DIM), BF16),
                   jax.ShapeDtypeStruct((bs, 1, LANES), F32)],
        compiler_params=_cparams(("parallel", "arbitrary")),
        name="fox_sample",
    )(pt_flat, q3, kn3, vn3, sm3, fb_row, *([cache_k4] * up), *([cache_v4] * up), *([cache_lf4] * up))


def _merge_kernel(x_ref, oa_ref, ob_ref, ga_ref, gb_ref, wba_ref, wbb_ref, wo_ref, g_ref, b_ref, o_ref):
    br_a = _dot(oa_ref[...], wba_ref[...])
    br_b = _dot(ob_ref[...], wbb_ref[...])
    merged = _sigmoid(ga_ref[...]) * br_a + _sigmoid(gb_ref[...]) * br_b
    y = _dot(merged.astype(BF16), wo_ref[...])
    o_ref[...] = _layer_norm(ALPHA * x_ref[...] + y, g_ref[...], b_ref[...])


def _merge_out(x, oa, ob, p, wba, wbb, wo, ln_g, ln_b, layer, tm):
    m = x.shape[0]
    const3 = lambda i: (layer, 0, 0)
    return pl.pallas_call(
        _merge_kernel,
        grid=(m // tm,),
        in_specs=[pl.BlockSpec((tm, D_MODEL), lambda i: (i, 0)),
                  pl.BlockSpec((tm, W_BRANCH), lambda i: (i, 0)),
                  pl.BlockSpec((tm, W_BRANCH), lambda i: (i, 0)),
                  pl.BlockSpec((tm, D_MODEL), lambda i: (i, OFF_GA // D_MODEL)),
                  pl.BlockSpec((tm, D_MODEL), lambda i: (i, OFF_GB // D_MODEL)),
                  pl.BlockSpec((None, W_BRANCH, D_MODEL), const3),
                  pl.BlockSpec((None, W_BRANCH, D_MODEL), const3),
                  pl.BlockSpec((None, D_MODEL, D_MODEL), const3),
                  pl.BlockSpec((None, 1, D_MODEL), const3),
                  pl.BlockSpec((None, 1, D_MODEL), const3)],
        out_specs=pl.BlockSpec((tm, D_MODEL), lambda i: (i, 0)),
        out_shape=jax.ShapeDtypeStruct((m, D_MODEL), F32),
        compiler_params=_cparams(("parallel",)),
        name="merge_out_ln1",
    )(x, oa, ob, p, p, wba, wbb, wo, ln_g, ln_b)


def _router_kernel(x_ref, w_ref, b_ref, gate_ref):
    logits = _dot_hi(x_ref[...], w_ref[...]) + b_ref[...]
    tm = logits.shape[0]
    lane = lax.broadcasted_iota(jnp.int32, (tm, LANES), 1).astype(F32)
    neg = -jnp.inf
    big = float(LANES)
    is_grp = lane < N_GROUPS
    lg = jnp.where(is_grp, logits, neg)
    m_g = jnp.max(lg, axis=-1, keepdims=True)
    gi = jnp.min(jnp.where(lg == m_g, lane, big), axis=-1, keepdims=True)
    w_g = 1.0 / jnp.sum(jnp.where(is_grp, jnp.exp(lg - m_g), 0.0), axis=-1, keepdims=True)
    e_lane = lane - ROUTER_E0
    in_grp = (e_lane >= gi * EXPERTS_PER_GROUP) & (e_lane < (gi + 1) * EXPERTS_PER_GROUP)
    le = jnp.where(in_grp, logits, neg)
    v1 = jnp.max(le, axis=-1, keepdims=True)
    i1 = jnp.min(jnp.where(le == v1, lane, big), axis=-1, keepdims=True)
    le2 = jnp.where(lane == i1, neg, le)
    v2 = jnp.max(le2, axis=-1, keepdims=True)
    i2 = jnp.min(jnp.where(le2 == v2, lane, big), axis=-1, keepdims=True)
    e2 = jnp.exp(v2 - v1)
    w1 = w_g / (1.0 + e2)
    w2 = w_g * e2 / (1.0 + e2)
    gate_ref[...] = jnp.where(lane == i1, w1, jnp.where(lane == i2, w2, 0.0))


def _router(x, w_r, b_r, layer, tm):
    m = x.shape[0]
    return pl.pallas_call(
        _router_kernel,
        grid=(m // tm,),
        in_specs=[pl.BlockSpec((tm, D_MODEL), lambda i: (i, 0)),
                  pl.BlockSpec((None, D_MODEL, LANES), lambda i: (layer, 0, 0)),
                  pl.BlockSpec((None, 1, LANES), lambda i: (layer, 0, 0))],
        out_specs=pl.BlockSpec((tm, LANES), lambda i: (i, 0)),
        out_shape=jax.ShapeDtypeStruct((m, LANES), F32),
        compiler_params=_cparams(("parallel",)),
        name="router",
    )(x, w_r, b_r)


def _moe_kernel(x_ref, gate_ref, wg_ref, wu_ref, wd_ref, g_ref, b_ref, o_ref, xb_scr, acc_scr):
    e = pl.program_id(1)

    @pl.when(e == 0)
    def _():
        xb_scr[...] = x_ref[...].astype(BF16)
        acc_scr[...] = jnp.zeros(acc_scr.shape, F32)

    lane = lax.broadcasted_iota(jnp.int32, gate_ref.shape, 1)
    gcol = jnp.sum(jnp.where(lane == ROUTER_E0 + e, gate_ref[...], 0.0), axis=-1, keepdims=True)

    @pl.when(jnp.max(gcol) > 0.0)
    def _():
        xb = xb_scr[...]
        hg = _dot(xb, wg_ref[...])
        hu = _dot(xb, wu_ref[...])
        h = (hg * _sigmoid(hg)) * hu * gcol
        acc_scr[...] += _dot(h.astype(BF16), wd_ref[...])

    @pl.when(e == pl.num_programs(1) - 1)
    def _():
        o_ref[...] = _layer_norm(ALPHA * x_ref[...] + acc_scr[...], g_ref[...], b_ref[...])


def _moe(x, gate, wg, wu, wd, ln_g, ln_b, layer, tm):
    m = x.shape[0]
    return pl.pallas_call(
        _moe_kernel,
        grid=(m // tm, N_EXPERTS),
        in_specs=[pl.BlockSpec((tm, D_MODEL), lambda i, e: (i, 0)),
                  pl.BlockSpec((tm, LANES), lambda i, e: (i, 0)),
                  pl.BlockSpec((None, None, D_MODEL, D_EXPERT), lambda i, e: (layer, e, 0, 0)),
                  pl.BlockSpec((None, None, D_MODEL, D_EXPERT), lambda i, e: (layer, e, 0, 0)),
                  pl.BlockSpec((None, None, D_EXPERT, D_MODEL), lambda i, e: (layer, e, 0, 0)),
                  pl.BlockSpec((None, 1, D_MODEL), lambda i, e: (layer, 0, 0)),
                  pl.BlockSpec((None, 1, D_MODEL), lambda i, e: (layer, 0, 0))],
        out_specs=pl.BlockSpec((tm, D_MODEL), lambda i, e: (i, 0)),
        out_shape=jax.ShapeDtypeStruct((m, D_MODEL), F32),
        scratch_shapes=[pltpu.VMEM((tm, D_MODEL), BF16), pltpu.VMEM((tm, D_MODEL), F32)],
        compiler_params=_cparams(("parallel", "arbitrary")),
        name="moe_ln2",
    )(x, gate, wg, wu, wd, ln_g, ln_b)


def _pack_w_in(w_in):
    s = 0
    qkv = w_in[..., s:s + C_CONV]; s += C_CONV
    z = w_in[..., s:s + W_BRANCH]; s += W_BRANCH
    beta = w_in[..., s:s + N_HEADS]; s += N_HEADS
    dec = w_in[..., s:s + N_HEADS]; s += N_HEADS
    qb = w_in[..., s:s + W_BRANCH]; s += W_BRANCH
    kb = w_in[..., s:s + W_BRANCH]; s += W_BRANCH
    vb = w_in[..., s:s + W_BRANCH]; s += W_BRANCH
    fb = w_in[..., s:s + N_HEADS]; s += N_HEADS
    ga = w_in[..., s:s + D_MODEL]; s += D_MODEL
    gb = w_in[..., s:s + D_MODEL]; s += D_MODEL
    pad = jnp.zeros(w_in.shape[:-1] + (LANES - 3 * N_HEADS,), w_in.dtype)
    w_main = jnp.concatenate([qkv, z, ga, gb, qb], axis=-1).astype(BF16)
    w_kvs = jnp.concatenate([kb, vb, beta, dec, fb, pad], axis=-1).astype(BF16)
    return w_main, w_kvs


def _lane_row(v, off):
    depth, n = v.shape
    return jnp.zeros((depth, 1, LANES), F32).at[:, 0, off:off + n].set(v.astype(F32))


def _pick_tile(m, pref):
    return pref if m % pref == 0 else m


def kernel(x_prompt, x_sample, cache_k, cache_v, cache_logf, state_conv, state_delta, page_table, w_in, conv_w, a_log, dt_bias, delta_norm_w, fox_forget_b, w_branch_a, w_branch_b, w_out, ln1_g, ln1_b, router_group_w, router_group_b, router_expert_w, router_expert_b, expert_w_gate, expert_w_up, expert_w_down, ln2_g, ln2_b):
    bp, seq, _ = x_prompt.shape
    bs = x_sample.shape[0]
    depth = w_in.shape[0]
    n_pool, _, page, _, _ = cache_k.shape

    w_main, w_kvs = _pack_w_in(w_in)
    wba = w_branch_a.astype(BF16)
    wbb = w_branch_b.astype(BF16)
    wo = w_out.astype(BF16)
    wg = expert_w_gate.astype(BF16)
    wu = expert_w_up.astype(BF16)
    wd = expert_w_down.astype(BF16)
    w_r = jnp.concatenate([router_group_w, router_expert_w,
                           jnp.zeros((depth, D_MODEL, LANES - N_GROUPS - N_EXPERTS), F32)], axis=-1)
    b_r = jnp.concatenate([router_group_b, router_expert_b,
                           jnp.zeros((depth, LANES - N_GROUPS - N_EXPERTS), F32)], axis=-1)[:, None, :]
    a_log_row = _lane_row(a_log, SM_DEC)
    dt_bias_row = _lane_row(dt_bias, SM_DEC)
    fb_row = _lane_row(fox_forget_b, SM_F)
    nw_row = delta_norm_w[:, None, :]
    ln1_g3, ln1_b3, ln2_g3, ln2_b3 = (t[:, None, :] for t in (ln1_g, ln1_b, ln2_g, ln2_b))
    cache_k4 = cache_k.reshape(n_pool, depth, page * N_HEADS, HEAD_DIM)
    cache_v4 = cache_v.reshape(n_pool, depth, page * N_HEADS, HEAD_DIM)
    cache_lf4 = cache_logf.reshape(n_pool, depth, page * N_HEADS // LANES, LANES)

    mp = bp * seq
    hp = x_prompt.reshape(mp, D_MODEL)
    hs = x_sample.reshape(bs, D_MODEL)
    tm_p = _pick_tile(seq, 512)
    tq = _pick_tile(seq, 512)

    k_all = v_all = None
    st_p = []
    st_s = []
    for l in range(depth):
        p = _proj(hp, w_main, l, tm_p, 1536)
        k_all, v_all, sm = _proj_kvs(hp, w_kvs, l, tm_p, bp, seq, depth, l, k_all, v_all)
        oa, s_fin = _delta_prompt(p, sm, conv_w, a_log_row, dt_bias_row, nw_row, l, bp, seq)
        logf, ft = _logf_prompt(sm, fb_row, l, bp, seq)
        ob = _fox_prompt(p, k_all, v_all, ft.reshape(bp, LANES, 1, seq), l, bp, seq, tq)
        x1 = _merge_out(hp, oa, ob, p, wba, wbb, wo, ln1_g3, ln1_b3, l, _pick_tile(mp, 256))
        gate = _router(x1, w_r, b_r, l, tm_p)
        hp = _moe(x1, gate, wg, wu, wd, ln2_g3, ln2_b3, l, tm_p)
        st_p.append((logf.reshape(bp, seq, LANES)[:, :, SM_F:SM_F + N_HEADS],
                     p.reshape(bp, seq, N_MAIN)[:, seq - (CONV_W - 1):, OFF_QKV:OFF_QKV + C_CONV],
                     s_fin))

        ps = _proj(hs, w_main, l, bs, 1536)
        ks, vs, sms = _proj_kvs(hs, w_kvs, l, bs, 1, bs, 1, 0)
        ks = ks.reshape(bs, N_HEADS, HEAD_DIM)
        vs = vs.reshape(bs, N_HEADS, HEAD_DIM)
        ps3 = ps.reshape(bs, 1, N_MAIN)
        sms3 = sms.reshape(bs, 1, LANES)
        oa_s, s_new = _delta_sample(ps3, sms3, state_conv, conv_w, a_log_row, dt_bias_row, nw_row, state_delta, l)
        q3 = ps[:, OFF_QB:OFF_QB + W_BRANCH].reshape(bs, N_HEADS, HEAD_DIM)
        ob_s, logf_s = _fox_sample(q3, ks, vs, sms3, fb_row, cache_k4, cache_v4, cache_lf4, page_table, l)
        x1s = _merge_out(hs, oa_s.reshape(bs, W_BRANCH), ob_s.reshape(bs, W_BRANCH), ps, wba, wbb, wo,
                         ln1_g3, ln1_b3, l, bs)
        gate_s = _router(x1s, w_r, b_r, l, bs)
        hs = _moe(x1s, gate_s, wg, wu, wd, ln2_g3, ln2_b3, l, bs)
        new_buf = jnp.concatenate([state_conv[:, l, 1:], ps3[:, :, OFF_QKV:OFF_QKV + C_CONV]], axis=1)
        st_s.append((ks[:, None], vs[:, None], logf_s[:, :, SM_F:SM_F + N_HEADS], new_buf, s_new))

    outs_p = tuple(jnp.stack([s[i] for s in st_p], axis=1) for i in range(3))
    outs_s = tuple(jnp.stack([s[i] for s in st_s], axis=1) for i in range(5))
    k_prompt = k_all.reshape(bp, depth, seq, N_HEADS, HEAD_DIM)
    v_prompt = v_all.reshape(bp, depth, seq, N_HEADS, HEAD_DIM)
    return (hp.reshape(bp, seq, D_MODEL), hs.reshape(bs, 1, D_MODEL), k_prompt, v_prompt) + outs_p + outs_s
```

```python
import functools

import jax
import jax.numpy as jnp
from jax import lax
from jax.experimental import pallas as pl
from jax.experimental.pallas import tpu as pltpu

F32 = jnp.float32
BF16 = jnp.bfloat16
HIGHEST = lax.Precision.HIGHEST

D_MODEL = 2048
N_HEADS = 8
HEAD_DIM = 128
W_BRANCH = N_HEADS * HEAD_DIM
C_CONV = 3 * W_BRANCH
CONV_W = 4
CHUNK = 64
N_GROUPS = 4
EXPERTS_PER_GROUP = 4
N_EXPERTS = 16
D_EXPERT = 512
ALPHA = 8.0 ** 0.25
LN_EPS = 1e-5
RMS_EPS = 1e-6
L2_EPS = 1e-6
LANES = 128
SUBLANES = 8

OFF_QKV = 0
OFF_Z = 3072
OFF_GA = 4096
OFF_GB = 6144
OFF_QB = 8192
N_MAIN = 9216
N_KVS = 2 * W_BRANCH + LANES
SM_BETA = 0
SM_DEC = 8
SM_F = 16
ROUTER_E0 = 4
PAGES_PER_STEP = 4
D_AUG = D_MODEL + LANES
GATHER_ROWS = 256
DELTA_GROUP = 4

VMEM_LIMIT = 56 * 1024 * 1024


def _cparams(sem):
    return pltpu.CompilerParams(dimension_semantics=sem, vmem_limit_bytes=VMEM_LIMIT)


def _sigmoid(x):
    return 1.0 / (1.0 + jnp.exp(-x))


def _softplus(x):
    return jnp.maximum(x, 0.0) + jnp.log1p(jnp.exp(-jnp.abs(x)))


def _log_sigmoid(x):
    return jnp.minimum(x, 0.0) - jnp.log1p(jnp.exp(-jnp.abs(x)))


def _dot(a, b):
    return jnp.dot(a, b, preferred_element_type=F32)


def _dot_hi(a, b):
    return jnp.dot(a, b, preferred_element_type=F32, precision=HIGHEST)


def _dot_nt(a, b, precision=None):
    return lax.dot_general(a, b, (((1,), (1,)), ((), ())), preferred_element_type=F32, precision=precision)


def _dot_tn(a, b, precision=None):
    return lax.dot_general(a, b, (((0,), (0,)), ((), ())), preferred_element_type=F32, precision=precision)


def _split2(a):
    hi = a.astype(BF16)
    return hi, (a - hi.astype(F32)).astype(BF16)


def _split3(a):
    hi = a.astype(BF16)
    r = a - hi.astype(F32)
    mid = r.astype(BF16)
    return hi, mid, (r - mid.astype(F32)).astype(BF16)


def _dot3(a, b):
    ah, al = _split2(a)
    bh, bl = _split2(b)
    return _dot(ah, bh) + (_dot(ah, bl) + _dot(al, bh))


def _dot_exact_rhs01(x, m01):
    hi, mid, lo = _split3(x)
    return _dot(hi, m01) + (_dot(mid, m01) + _dot(lo, m01))


def _layer_norm(y, g, b):
    mu = jnp.mean(y, axis=-1, keepdims=True)
    yc = y - mu
    var = jnp.mean(yc * yc, axis=-1, keepdims=True)
    return yc * lax.rsqrt(var + LN_EPS) * g + b


def _proj_kernel(x_ref, w_ref, o_ref):
    o_ref[...] = _dot(x_ref[...].astype(BF16), w_ref[...])


def _proj(x, w_main, layer, tm, tn):
    m, k = x.shape
    n = w_main.shape[-1]
    return pl.pallas_call(
        _proj_kernel,
        grid=(n // tn, m // tm),
        in_specs=[pl.BlockSpec((tm, k), lambda j, i: (i, 0)),
                  pl.BlockSpec((None, k, tn), lambda j, i: (layer, 0, j))],
        out_specs=pl.BlockSpec((tm, tn), lambda j, i: (i, j)),
        out_shape=jax.ShapeDtypeStruct((m, n), F32),
        compiler_params=_cparams(("parallel", "parallel")),
        name="proj_in",
    )(x, w_main)


def _proj_kvs_kernel(x_ref, w_ref, *refs):
    k_ref, v_ref, sm_ref = refs[-3:]
    r = _dot(x_ref[...].astype(BF16), w_ref[...])
    k_ref[...] = r[:, :W_BRANCH]
    v_ref[...] = r[:, W_BRANCH:2 * W_BRANCH]
    sm_ref[...] = r[:, 2 * W_BRANCH:]


def _proj_kvs(x, w_kvs, layer, tm, batch, seq, slots, slot, k_all=None, v_all=None):
    m, k = x.shape
    nt = seq // tm
    kv_spec = pl.BlockSpec((None, None, tm, W_BRANCH), lambda i: (i // nt, slot, i % nt, 0))
    kv_shape = jax.ShapeDtypeStruct((batch, slots, seq, W_BRANCH), F32)
    in_specs = [pl.BlockSpec((tm, k), lambda i: (i, 0)),
                pl.BlockSpec((None, k, N_KVS), lambda i: (layer, 0, 0))]
    args = [x, w_kvs]
    aliases = {}
    if k_all is not None:
        in_specs += [pl.BlockSpec(memory_space=pl.ANY), pl.BlockSpec(memory_space=pl.ANY)]
        args += [k_all, v_all]
        aliases = {2: 0, 3: 1}
    return pl.pallas_call(
        _proj_kvs_kernel,
        grid=(m // tm,),
        in_specs=in_specs,
        out_specs=[kv_spec, kv_spec, pl.BlockSpec((tm, LANES), lambda i: (i, 0))],
        out_shape=[kv_shape, kv_shape, jax.ShapeDtypeStruct((m, LANES), F32)],
        input_output_aliases=aliases,
        compiler_params=_cparams(("parallel",)),
        name="proj_kvs",
    )(*args)


def _delta_kernel(x_ref, z_ref, sm_ref, w_ref, alog_ref, dtb_ref, nw_ref, o_ref, sout_ref, xx, s_scr, *, chunk):
    c = chunk
    n = pl.program_id(1)

    @pl.when(n == 0)
    def _():
        xx[0:SUBLANES, :] = jnp.zeros((SUBLANES, C_CONV), F32)
        s_scr[...] = jnp.zeros(s_scr.shape, F32)

    xx[SUBLANES:SUBLANES + c, :] = x_ref[...]
    acc = w_ref[CONV_W - 1:CONV_W, :] * x_ref[...]
    for j in range(CONV_W - 1):
        off = SUBLANES - (CONV_W - 1) + j
        acc = acc + w_ref[j:j + 1, :] * xx[off:off + c, :]
    tail = xx[c:c + SUBLANES, :]
    xx[0:SUBLANES, :] = tail
    y = acc * _sigmoid(acc)

    sm = sm_ref[...]
    beta_all = _sigmoid(sm)
    g_all = -jnp.exp(alog_ref[...]) * _softplus(sm + dtb_ref[...])
    r1 = lax.broadcasted_iota(jnp.int32, (c, c), 0)
    c1 = lax.broadcasted_iota(jnp.int32, (c, c), 1)
    lower01 = jnp.where(r1 >= c1, 1.0, 0.0).astype(BF16)
    upper01 = jnp.where(r1 <= c1, 1.0, 0.0).astype(BF16)
    g_hi, g_mid, g_lo = _split3(g_all)
    cum_col = _dot(jnp.concatenate([lower01] * 3, axis=1), jnp.concatenate([g_hi, g_mid, g_lo], axis=0))
    cum_row = _dot_tn(jnp.concatenate([g_hi, g_mid, g_lo], axis=0), jnp.concatenate([upper01] * 3, axis=0))

    gw = DELTA_GROUP * c
    sw = DELTA_GROUP * HEAD_DIM
    row = lax.broadcasted_iota(jnp.int32, (gw, gw), 0)
    col = lax.broadcasted_iota(jnp.int32, (gw, gw), 1)
    same = (row // c) == (col // c)
    tri_incl = same & (row >= col)
    tri_strict = same & (row > col)
    own_block = (lax.broadcasted_iota(jnp.int32, (gw, sw), 0) // c) == (lax.broadcasted_iota(jnp.int32, (gw, sw), 1) // HEAD_DIM)

    def stack(ref_or_val, base):
        return jnp.concatenate([ref_or_val[:, base + h * HEAD_DIM:base + (h + 1) * HEAD_DIM] for h in heads], axis=0)

    for g in range(N_HEADS // DELTA_GROUP):
        heads = range(g * DELTA_GROUP, (g + 1) * DELTA_GROUP)
        gcol = jnp.concatenate([jnp.broadcast_to(cum_col[:, SM_DEC + h:SM_DEC + h + 1], (c, gw)) for h in heads], axis=0)
        grow = jnp.broadcast_to(jnp.concatenate([cum_row[SM_DEC + h:SM_DEC + h + 1, :] for h in heads], axis=1), (gw, gw))
        bcol = jnp.concatenate([jnp.broadcast_to(beta_all[:, SM_BETA + h:SM_BETA + h + 1], (c, gw)) for h in heads], axis=0)
        gend = jnp.concatenate([jnp.broadcast_to(cum_col[c - 1:c, SM_DEC + h:SM_DEC + h + 1], (c, HEAD_DIM)) for h in heads], axis=0)
        gend_row = jnp.concatenate([jnp.broadcast_to(cum_col[c - 1:c, SM_DEC + h:SM_DEC + h + 1], (1, HEAD_DIM)) for h in heads], axis=1)
        decay = jnp.where(tri_incl, jnp.exp(jnp.where(tri_incl, gcol - grow, 0.0)), 0.0)
        gcol1 = gcol[:, :HEAD_DIM]
        bcol1 = bcol[:, :HEAD_DIM]

        q = stack(y, 0)
        k = stack(y, W_BRANCH)
        v = stack(y, 2 * W_BRANCH)
        q = q * lax.rsqrt(jnp.sum(q * q, axis=-1, keepdims=True) + L2_EPS) * (HEAD_DIM ** -0.5)
        k = k * lax.rsqrt(jnp.sum(k * k, axis=-1, keepdims=True) + L2_EPS)
        qb = q.astype(BF16)
        kb = k.astype(BF16)

        qkk = _dot_nt(jnp.concatenate([qb, kb], axis=0), kb)
        qk = qkk[:gw] * decay
        xf = jnp.where(tri_strict, -(bcol * decay * qkk[gw:]), 0.0)
        xb = xf.astype(BF16)
        e_g = jnp.exp(gcol1)
        sol = jnp.concatenate([(bcol1 * e_g) * k, bcol1 * v], axis=1)
        p = 1
        while p < c:
            if p == 1 and 2 * p < c:
                s_hi, s_lo = _split2(sol)
                x_hi, x_lo = _split2(xf)
                hi_row = jnp.concatenate([s_hi, x_hi], axis=1)
                prod = _dot(jnp.concatenate([x_hi, x_hi, x_lo], axis=1),
                            jnp.concatenate([hi_row, jnp.concatenate([s_lo, x_lo], axis=1), hi_row], axis=0))
                sol = sol + prod[:, :2 * HEAD_DIM]
                xb = prod[:, 2 * HEAD_DIM:].astype(BF16)
            elif 2 * p < c:
                prod = _dot(xb, jnp.concatenate([sol.astype(BF16), xb], axis=1))
                sol = sol + prod[:, :2 * HEAD_DIM]
                xb = prod[:, 2 * HEAD_DIM:].astype(BF16)
            else:
                sol = sol + _dot(xb, sol.astype(BF16))
            p *= 2
        w_blk = sol[:, :HEAD_DIM]
        uv_blk = sol[:, HEAD_DIM:]
        q_dec = q * e_g
        k_dec = k * jnp.exp(gend - gcol1)

        s_old = s_scr[g]
        res = _dot(jnp.concatenate([w_blk, q_dec], axis=0).astype(BF16), s_old.astype(BF16))
        ws = jnp.concatenate([res[i * c:(i + 1) * c, i * HEAD_DIM:(i + 1) * HEAD_DIM] for i in range(DELTA_GROUP)], axis=0)
        qs = jnp.concatenate([res[gw + i * c:gw + (i + 1) * c, i * HEAD_DIM:(i + 1) * HEAD_DIM]
                              for i in range(DELTA_GROUP)], axis=0)
        u = uv_blk - ws
        o = qs + _dot(qk.astype(BF16), u.astype(BF16))
        u_blocks = jnp.where(own_block, jnp.concatenate([u] * DELTA_GROUP, axis=1), 0.0).astype(BF16)
        s_scr[g] = jnp.exp(gend_row) * s_old + _dot_tn(k_dec.astype(BF16), u_blocks)

        zz = stack(z_ref, 0)
        o = o * lax.rsqrt(jnp.mean(o * o, axis=-1, keepdims=True) + RMS_EPS) * nw_ref[...] * (zz * _sigmoid(zz))
        for i, h in enumerate(heads):
            o_ref[:, h * HEAD_DIM:(h + 1) * HEAD_DIM] = o[i * c:(i + 1) * c, :].astype(o_ref.dtype)

    @pl.when(n == pl.num_programs(1) - 1)
    def _():
        for h in range(N_HEADS):
            i = h % DELTA_GROUP
            sout_ref[h] = s_scr[h // DELTA_GROUP, :, i * HEAD_DIM:(i + 1) * HEAD_DIM]


def _delta_prompt(p, sm, conv_w, a_log_row, dt_bias_row, norm_w_row, layer, batch, seq):
    c = CHUNK if seq % CHUNK == 0 else seq
    nch = seq // c
    rowspec = pl.BlockSpec((None, 1, LANES), lambda b, n: (layer, 0, 0))
    return pl.pallas_call(
        functools.partial(_delta_kernel, chunk=c),
        grid=(batch, nch),
        in_specs=[pl.BlockSpec((c, C_CONV), lambda b, n: (b * nch + n, OFF_QKV // C_CONV)),
                  pl.BlockSpec((c, W_BRANCH), lambda b, n: (b * nch + n, OFF_Z // W_BRANCH)),
                  pl.BlockSpec((c, LANES), lambda b, n: (b * nch + n, 0)),
                  pl.BlockSpec((None, CONV_W, C_CONV), lambda b, n: (layer, 0, 0)),
                  rowspec, rowspec, rowspec],
        out_specs=[pl.BlockSpec((c, W_BRANCH), lambda b, n: (b * nch + n, 0)),
                   pl.BlockSpec((None, N_HEADS, HEAD_DIM, HEAD_DIM), lambda b, n: (b, 0, 0, 0))],
        out_shape=[jax.ShapeDtypeStruct((batch * seq, W_BRANCH), BF16),
                   jax.ShapeDtypeStruct((batch, N_HEADS, HEAD_DIM, HEAD_DIM), F32)],
        scratch_shapes=[pltpu.VMEM((c + SUBLANES, C_CONV), F32),
                        pltpu.VMEM((N_HEADS // DELTA_GROUP, HEAD_DIM, DELTA_GROUP * HEAD_DIM), F32)],
        compiler_params=_cparams(("parallel", "arbitrary")),
        name="delta_prompt",
    )(p, p, sm, conv_w, a_log_row, dt_bias_row, norm_w_row)


def _delta_step_kernel(x_ref, z_ref, sm_ref, cbuf_ref, w_ref, alog_ref, dtb_ref, nw_ref, s0_ref,
                       o_ref, sout_ref):
    x_new = x_ref[...]
    acc = w_ref[CONV_W - 1:CONV_W, :] * x_new
    for j in range(CONV_W - 1):
        acc = acc + w_ref[j:j + 1, :] * cbuf_ref[j:j + 1, :]
    y = acc * _sigmoid(acc)
    sm = sm_ref[...]
    lane = lax.broadcasted_iota(jnp.int32, (1, LANES), 1)
    beta_all = _sigmoid(sm)
    g_all = -jnp.exp(alog_ref[...]) * _softplus(sm + dtb_ref[...])
    rows8 = lax.broadcasted_iota(jnp.int32, (SUBLANES, HEAD_DIM), 0)

    def pad8(r):
        return jnp.where(rows8 == 0, r, 0.0)

    for h in range(N_HEADS):
        g = jnp.sum(jnp.where(lane == SM_DEC + h, g_all, 0.0), axis=-1, keepdims=True)
        beta = jnp.sum(jnp.where(lane == SM_BETA + h, beta_all, 0.0), axis=-1, keepdims=True)
        q = y[:, h * HEAD_DIM:(h + 1) * HEAD_DIM]
        k = y[:, W_BRANCH + h * HEAD_DIM:W_BRANCH + (h + 1) * HEAD_DIM]
        v = y[:, 2 * W_BRANCH + h * HEAD_DIM:2 * W_BRANCH + (h + 1) * HEAD_DIM]
        q = q * lax.rsqrt(jnp.sum(q * q, axis=-1, keepdims=True) + L2_EPS) * (HEAD_DIM ** -0.5)
        k = k * lax.rsqrt(jnp.sum(k * k, axis=-1, keepdims=True) + L2_EPS)
        e_g = jnp.exp(g)
        s_old = s0_ref[h]
        w_row = (beta * e_g) * k
        proj = _dot_hi(jnp.concatenate([pad8(w_row), pad8(q * e_g)], axis=0), s_old)
        u = beta * v - proj[0:1, :]
        qk = jnp.sum(q * k, axis=-1, keepdims=True)
        o = proj[SUBLANES:SUBLANES + 1, :] + qk * u
        s_new = e_g * s_old + _dot_tn(pad8(k), pad8(u), HIGHEST)
        sout_ref[h] = s_new
        zz = z_ref[:, h * HEAD_DIM:(h + 1) * HEAD_DIM]
        o = o * lax.rsqrt(jnp.mean(o * o, axis=-1, keepdims=True) + RMS_EPS) * nw_ref[...] * (zz * _sigmoid(zz))
        o_ref[:, h * HEAD_DIM:(h + 1) * HEAD_DIM] = o.astype(o_ref.dtype)


def _delta_sample(p3, sm3, state_conv, conv_w, a_log_row, dt_bias_row, norm_w_row, state_delta, layer):
    bs = p3.shape[0]
    rowspec = pl.BlockSpec((None, 1, LANES), lambda b: (layer, 0, 0))
    return pl.pallas_call(
        _delta_step_kernel,
        grid=(bs,),
        in_specs=[pl.BlockSpec((None, 1, C_CONV), lambda b: (b, 0, OFF_QKV // C_CONV)),
                  pl.BlockSpec((None, 1, W_BRANCH), lambda b: (b, 0, OFF_Z // W_BRANCH)),
                  pl.BlockSpec((None, 1, LANES), lambda b: (b, 0, 0)),
                  pl.BlockSpec((None, None, CONV_W - 1, C_CONV), lambda b: (b, layer, 0, 0)),
                  pl.BlockSpec((None, CONV_W, C_CONV), lambda b: (layer, 0, 0)),
                  rowspec, rowspec, rowspec,
                  pl.BlockSpec((None, None, N_HEADS, HEAD_DIM, HEAD_DIM), lambda b: (b, layer, 0, 0, 0))],
        out_specs=[pl.BlockSpec((None, 1, W_BRANCH), lambda b: (b, 0, 0)),
                   pl.BlockSpec((None, N_HEADS, HEAD_DIM, HEAD_DIM), lambda b: (b, 0, 0, 0))],
        out_shape=[jax.ShapeDtypeStruct((bs, 1, W_BRANCH), BF16),
                   jax.ShapeDtypeStruct((bs, N_HEADS, HEAD_DIM, HEAD_DIM), F32)],
        compiler_params=_cparams(("parallel",)),
        name="delta_sample",
    )(p3, p3, sm3, state_conv, conv_w, a_log_row, dt_bias_row, norm_w_row, state_delta)


def _logf_kernel(sm_ref, fb_ref, logf_ref, fcum_ref, carry):
    i = pl.program_id(1)

    @pl.when(i == 0)
    def _():
        carry[...] = jnp.zeros(carry.shape, F32)

    logf = _log_sigmoid(sm_ref[...] + fb_ref[...])
    logf_ref[...] = logf
    t = logf.shape[0]
    row = lax.broadcasted_iota(jnp.int32, (t, t), 0)
    col = lax.broadcasted_iota(jnp.int32, (t, t), 1)
    lower01 = jnp.where(row >= col, 1.0, 0.0).astype(BF16)
    hi, mid, lo = _split3(logf)
    cum = _dot(jnp.concatenate([lower01] * 3, axis=1), jnp.concatenate([hi, mid, lo], axis=0)) + carry[...]
    fcum_ref[...] = cum
    carry[...] = cum[t - 1:t, :]


def _logf_prompt(sm, fb_row, layer, batch, seq):
    t = LANES
    nblk = seq // t
    return pl.pallas_call(
        _logf_kernel,
        grid=(batch, nblk),
        in_specs=[pl.BlockSpec((t, LANES), lambda b, i: (b * nblk + i, 0)),
                  pl.BlockSpec((None, 1, LANES), lambda b, i: (layer, 0, 0))],
        out_specs=[pl.BlockSpec((t, LANES), lambda b, i: (b * nblk + i, 0)),
                   pl.BlockSpec((t, LANES), lambda b, i: (b * nblk + i, 0))],
        out_shape=[jax.ShapeDtypeStruct((batch * seq, LANES), F32),
                   jax.ShapeDtypeStruct((batch * seq, LANES), F32)],
        scratch_shapes=[pltpu.VMEM((1, LANES), F32)],
        compiler_params=_cparams(("parallel", "arbitrary")),
        name="logf_prompt",
    )(sm, fb_row)


def _fox_kernel(q_ref, k_ref, v_ref, f_ref, o_ref, ka_scr, vt_scr, m_scr, l_scr, acc_scr, st_scr, *, tq, tk, prep):
    h = pl.program_id(1)
    i = pl.program_id(2)
    seq = k_ref.shape[0]

    @pl.when(i == 0)
    def _():
        lane = lax.broadcasted_iota(jnp.int32, (prep, LANES), 1)
        for r0 in range(0, seq, prep):
            rs = slice(r0, r0 + prep)
            ka_scr[rs, 0:HEAD_DIM] = k_ref[rs, :].astype(BF16)
            f_col = jnp.sum(jnp.where(lane == SM_F + h, f_ref[rs, :], 0.0), axis=-1, keepdims=True)
            f_all = jnp.broadcast_to(f_col, (prep, LANES))
            f_hi = f_all.astype(BF16).astype(F32)
            rem = f_all - f_hi
            f_mid = rem.astype(BF16).astype(F32)
            f_lo = rem - f_mid
            aug = jnp.where(lane == 0, f_hi, jnp.where(lane == 1, f_mid, jnp.where(lane == 2, f_lo, 0.0)))
            ka_scr[rs, HEAD_DIM:2 * HEAD_DIM] = aug.astype(BF16)
            vt_scr[:, rs] = v_ref[rs, :].T.astype(BF16)

    q_t = (q_ref[...] * (HEAD_DIM ** -0.5)).T
    sub = lax.broadcasted_iota(jnp.int32, (HEAD_DIM, tq), 0)
    qa = jnp.concatenate([q_t.astype(BF16), jnp.where(sub < 3, -1.0, 0.0).astype(BF16)], axis=0)
    m_scr[...] = jnp.full(m_scr.shape, -jnp.inf, F32)
    l_scr[...] = jnp.zeros(l_scr.shape, F32)
    acc_scr[...] = jnp.zeros(acc_scr.shape, F32)

    def scores(j):
        start = pl.multiple_of(j * tk, tk)
        return _dot(ka_scr[pl.ds(start, tk), :], qa)

    def update(j, st):
        start = pl.multiple_of(j * tk, tk)
        m_old = m_scr[...]
        m_new = jnp.maximum(m_old, jnp.max(st, axis=0, keepdims=True))
        alpha = jnp.exp(m_old - m_new)
        p = jnp.exp(st - m_new)
        l_scr[...] = alpha * l_scr[...] + jnp.sum(p, axis=0, keepdims=True)
        acc_scr[...] = alpha * acc_scr[...] + _dot(vt_scr[:, pl.ds(start, tk)], p.astype(BF16))
        m_scr[...] = m_new

    st_scr[...] = scores(0)

    def body(j, carry):
        st_next = scores(j + 1)
        update(j, st_scr[...])
        st_scr[...] = st_next
        return carry

    lax.fori_loop(0, i, body, 0)
    row = lax.broadcasted_iota(jnp.int32, (tk, tq), 0)
    col = lax.broadcasted_iota(jnp.int32, (tk, tq), 1)
    update(i, jnp.where(col >= row, st_scr[...], -jnp.inf))
    o_ref[...] = (acc_scr[...] / l_scr[...]).T.astype(o_ref.dtype)


def _fox_prompt(p, k_all, v_all, fcum, layer, batch, seq, tq):
    nq = seq // tq
    kv_spec = pl.BlockSpec((None, None, seq, HEAD_DIM), lambda b, h, i: (b, layer, 0, h))
    return pl.pallas_call(
        functools.partial(_fox_kernel, tq=tq, tk=tq, prep=_pick_tile(seq, 512)),
        grid=(batch, N_HEADS, nq),
        in_specs=[pl.BlockSpec((tq, HEAD_DIM), lambda b, h, i: (b * nq + i, OFF_QB // HEAD_DIM + h)),
                  kv_spec, kv_spec,
                  pl.BlockSpec((seq, LANES), lambda b, h, i: (b, 0))],
        out_specs=pl.BlockSpec((tq, HEAD_DIM), lambda b, h, i: (b * nq + i, h)),
        out_shape=jax.ShapeDtypeStruct((batch * seq, W_BRANCH), BF16),
        scratch_shapes=[pltpu.VMEM((seq, 2 * HEAD_DIM), BF16), pltpu.VMEM((HEAD_DIM, seq), BF16),
                        pltpu.VMEM((1, tq), F32), pltpu.VMEM((1, tq), F32), pltpu.VMEM((HEAD_DIM, tq), F32),
                        pltpu.VMEM((tq, tq), F32)],
        compiler_params=_cparams(("parallel", "parallel", "arbitrary")),
        name="fox_prompt",
    )(p, k_all, v_all, fcum)


def _fox_decode_kernel(pt_ref, q_ref, kn_ref, vn_ref, sm_ref, fb_ref, *refs, up):
    del pt_ref
    k_refs = refs[0:up]
    v_refs = refs[up:2 * up]
    lf_refs = refs[2 * up:3 * up]
    o_ref, logf_ref, kb_scr, vb_scr, base, m_scr, l_scr, acc_scr = refs[3 * up:]
    j = pl.program_id(1)
    rows = k_refs[0].shape[0]
    nsub = rows // LANES
    nrow = up * nsub

    @pl.when(j == 0)
    def _():
        base[...] = jnp.zeros(base.shape, F32)
        m_scr[...] = jnp.full(m_scr.shape, -jnp.inf, F32)
        l_scr[...] = jnp.zeros(l_scr.shape, F32)
        acc_scr[...] = jnp.zeros(acc_scr.shape, F32)

    for u in range(up):
        kb_scr[u * rows:(u + 1) * rows, :] = k_refs[u][...].astype(BF16)
        vb_scr[u * rows:(u + 1) * rows, :] = v_refs[u][...].astype(BF16)

    ci = lax.broadcasted_iota(jnp.int32, (LANES, LANES), 0)
    cj = lax.broadcasted_iota(jnp.int32, (LANES, LANES), 1)
    same_head = (ci % N_HEADS) == (cj % N_HEADS)
    cum01 = jnp.where(same_head & (ci // N_HEADS <= cj // N_HEADS), 1.0, 0.0).astype(BF16)
    tot01 = jnp.where(same_head, 1.0, 0.0).astype(BF16)
    both01 = jnp.concatenate([cum01, tot01], axis=1)
    ri = lax.broadcasted_iota(jnp.int32, (nrow, nrow), 0)
    rj = lax.broadcasted_iota(jnp.int32, (nrow, nrow), 1)
    below01 = jnp.where(ri > rj, 1.0, 0.0).astype(BF16)
    lf = jnp.concatenate([lf_refs[u][...] for u in range(up)], axis=0)
    lf_hi, lf_mid, lf_lo = _split3(lf)
    r = _dot(jnp.concatenate([lf_hi, lf_mid, lf_lo], axis=1), jnp.concatenate([both01] * 3, axis=0))
    within = r[:, :LANES]
    totals = r[:, LANES:]
    t_hi, t_mid, t_lo = _split3(totals)
    before = _dot(jnp.concatenate([below01] * 3, axis=1), jnp.concatenate([t_hi, t_mid, t_lo], axis=0))
    f_rows = within + before + base[...]
    base[...] = base[...] + jnp.sum(totals, axis=0, keepdims=True)

    qb = (q_ref[...] * (HEAD_DIM ** -0.5)).astype(BF16)
    s_all = _dot_nt(qb, kb_scr[...])
    own_head = (lax.broadcasted_iota(jnp.int32, (N_HEADS, LANES), 1) % N_HEADS
                == lax.broadcasted_iota(jnp.int32, (N_HEADS, LANES), 0))
    s = jnp.concatenate([jnp.where(own_head, s_all[:, a * LANES:(a + 1) * LANES] - f_rows[a:a + 1, :], -jnp.inf)
                         for a in range(nrow)], axis=1)
    m_old = m_scr[...]
    m_new = jnp.maximum(m_old, jnp.max(s, axis=-1, keepdims=True))
    alpha = jnp.exp(m_old - m_new)
    p = jnp.exp(s - m_new)
    l_scr[...] = alpha * l_scr[...] + jnp.sum(p, axis=-1, keepdims=True)
    acc_scr[...] = alpha * acc_scr[...] + _dot(p.astype(BF16), vb_scr[...])
    m_scr[...] = m_new

    @pl.when(j == pl.num_programs(1) - 1)
    def _():
        lane8 = lax.broadcasted_iota(jnp.int32, (N_HEADS, LANES), 1)
        sub8 = lax.broadcasted_iota(jnp.int32, (N_HEADS, LANES), 0)
        logf_row = _log_sigmoid(sm_ref[...] + fb_ref[...])
        logf_ref[...] = logf_row
        logf_new = jnp.sum(jnp.where(lane8 == SM_F + sub8, logf_row, 0.0), axis=-1, keepdims=True)
        f_past = jnp.sum(jnp.where(lane8 == sub8, base[...], 0.0), axis=-1, keepdims=True)
        s_new = jnp.sum(q_ref[...] * (HEAD_DIM ** -0.5) * kn_ref[...], axis=-1, keepdims=True) - (f_past + logf_new)
        m_prev = m_scr[...]
        m_fin = jnp.maximum(m_prev, s_new)
        a_fin = jnp.exp(m_prev - m_fin)
        p_new = jnp.exp(s_new - m_fin)
        l_fin = a_fin * l_scr[...] + p_new
        o_ref[...] = ((a_fin * acc_scr[...] + p_new * vn_ref[...]) / l_fin).astype(o_ref.dtype)


def _fox_sample(q3, kn3, vn3, sm3, fb_row, cache_k4, cache_v4, cache_lf4, page_table, layer):
    bs, n_pages = page_table.shape
    rows = cache_k4.shape[2]
    nsub = cache_lf4.shape[2]
    up = PAGES_PER_STEP if n_pages % PAGES_PER_STEP == 0 else 1
    pt_flat = page_table.reshape(-1)
    hspec = pl.BlockSpec((None, N_HEADS, HEAD_DIM), lambda b, j, pt: (b, 0, 0))

    def cspec(u, second, last):
        return pl.BlockSpec((None, None, second, last),
                            lambda b, j, pt: (pt[b * n_pages + j * up + u], layer, 0, 0))

    grid_spec = pltpu.PrefetchScalarGridSpec(
        num_scalar_prefetch=1,
        grid=(bs, n_pages // up),
        in_specs=[hspec, hspec, hspec,
                  pl.BlockSpec((None, 1, LANES), lambda b, j, pt: (b, 0, 0)),
                  pl.BlockSpec((None, 1, LANES), lambda b, j, pt: (layer, 0, 0))]
                 + [cspec(u, rows, HEAD_DIM) for u in range(up)]
                 + [cspec(u, rows, HEAD_DIM) for u in range(up)]
                 + [cspec(u, nsub, LANES) for u in range(up)],
        out_specs=[hspec, pl.BlockSpec((None, 1, LANES), lambda b, j, pt: (b, 0, 0))],
        scratch_shapes=[pltpu.VMEM((up * rows, HEAD_DIM), BF16), pltpu.VMEM((up * rows, HEAD_DIM), BF16),
                        pltpu.VMEM((1, LANES), F32), pltpu.VMEM((N_HEADS, 1), F32),
                        pltpu.VMEM((N_HEADS, 1), F32), pltpu.VMEM((N_HEADS, HEAD_DIM), F32)])
    return pl.pallas_call(
        functools.partial(_fox_decode_kernel, up=up),
        grid_spec=grid_spec,
        out_shape=[jax.ShapeDtypeStruct((bs, N_HEADS, HEAD_DIM), BF16),
                   jax.ShapeDtypeStruct((bs, 1, LANES), F32)],
        compiler_params=_cparams(("parallel", "arbitrary")),
        name="fox_sample",
    )(pt_flat, q3, kn3, vn3, sm3, fb_row, *([cache_k4] * up), *([cache_v4] * up), *([cache_lf4] * up))


def _merge_kernel(x_ref, oa_ref, ob_ref, ga_ref, gb_ref, wba_ref, wbb_ref, wo_ref, g_ref, b_ref, o_ref):
    br_a = _dot(oa_ref[...], wba_ref[...])
    br_b = _dot(ob_ref[...], wbb_ref[...])
    merged = _sigmoid(ga_ref[...]) * br_a + _sigmoid(gb_ref[...]) * br_b
    y = _dot(merged.astype(BF16), wo_ref[...])
    o_ref[...] = _layer_norm(ALPHA * x_ref[...] + y, g_ref[...], b_ref[...])


def _merge_out(x, oa, ob, p, wba, wbb, wo, ln_g, ln_b, layer, tm):
    m = x.shape[0]
    const3 = lambda i: (layer, 0, 0)
    return pl.pallas_call(
        _merge_kernel,
        grid=(m // tm,),
        in_specs=[pl.BlockSpec((tm, D_MODEL), lambda i: (i, 0)),
                  pl.BlockSpec((tm, W_BRANCH), lambda i: (i, 0)),
                  pl.BlockSpec((tm, W_BRANCH), lambda i: (i, 0)),
                  pl.BlockSpec((tm, D_MODEL), lambda i: (i, OFF_GA // D_MODEL)),
                  pl.BlockSpec((tm, D_MODEL), lambda i: (i, OFF_GB // D_MODEL)),
                  pl.BlockSpec((None, W_BRANCH, D_MODEL), const3),
                  pl.BlockSpec((None, W_BRANCH, D_MODEL), const3),
                  pl.BlockSpec((None, D_MODEL, D_MODEL), const3),
                  pl.BlockSpec((None, 1, D_MODEL), const3),
                  pl.BlockSpec((None, 1, D_MODEL), const3)],
        out_specs=pl.BlockSpec((tm, D_MODEL), lambda i: (i, 0)),
        out_shape=jax.ShapeDtypeStruct((m, D_MODEL), F32),
        compiler_params=_cparams(("parallel",)),
        name="merge_out_ln1",
    )(x, oa, ob, p, p, wba, wbb, wo, ln_g, ln_b)


def _router_kernel(x_ref, w_ref, b_ref, o_ref):
    logits = _dot_hi(x_ref[...], w_ref[...]) + b_ref[...]
    tm = logits.shape[0]
    lane = lax.broadcasted_iota(jnp.int32, (tm, LANES), 1).astype(F32)
    neg = -jnp.inf
    big = float(LANES)
    is_grp = lane < N_GROUPS
    lg = jnp.where(is_grp, logits, neg)
    m_g = jnp.max(lg, axis=-1, keepdims=True)
    gi = jnp.min(jnp.where(lg == m_g, lane, big), axis=-1, keepdims=True)
    w_g = 1.0 / jnp.sum(jnp.where(is_grp, jnp.exp(lg - m_g), 0.0), axis=-1, keepdims=True)
    e_lane = lane - ROUTER_E0
    in_grp = (e_lane >= gi * EXPERTS_PER_GROUP) & (e_lane < (gi + 1) * EXPERTS_PER_GROUP)
    le = jnp.where(in_grp, logits, neg)
    v1 = jnp.max(le, axis=-1, keepdims=True)
    i1 = jnp.min(jnp.where(le == v1, lane, big), axis=-1, keepdims=True)
    le2 = jnp.where(lane == i1, neg, le)
    v2 = jnp.max(le2, axis=-1, keepdims=True)
    i2 = jnp.min(jnp.where(le2 == v2, lane, big), axis=-1, keepdims=True)
    e2 = jnp.exp(v2 - v1)
    w1 = w_g / (1.0 + e2)
    w2 = w_g * e2 / (1.0 + e2)
    o_ref[:, :D_MODEL] = x_ref[...]
    o_ref[:, D_MODEL:] = jnp.where(lane == i1, w1, jnp.where(lane == i2, w2, jnp.where(lane == 0.0, gi, 0.0)))


def _router(x, w_r, b_r, layer, tm):
    m = x.shape[0]
    return pl.pallas_call(
        _router_kernel,
        grid=(m // tm,),
        in_specs=[pl.BlockSpec((tm, D_MODEL), lambda i: (i, 0)),
                  pl.BlockSpec((None, D_MODEL, LANES), lambda i: (layer, 0, 0)),
                  pl.BlockSpec((None, 1, LANES), lambda i: (layer, 0, 0))],
        out_specs=pl.BlockSpec((tm, D_AUG), lambda i: (i, 0)),
        out_shape=jax.ShapeDtypeStruct((m, D_AUG), F32),
        compiler_params=_cparams(("parallel",)),
        name="router",
    )(x, w_r, b_r)


def _gather_kernel(idx_ref, x_hbm, o_ref, sem):
    tb = o_ref.shape[0]
    base = pl.program_id(0) * tb

    def row_copy(r, src_row):
        return pltpu.make_async_copy(x_hbm.at[pl.ds(src_row, 1)], o_ref.at[pl.ds(r, 1)], sem)

    def start(r, carry):
        row_copy(r, idx_ref[base + r]).start()
        return carry

    def wait(r, carry):
        row_copy(r, 0).wait()
        return carry

    lax.fori_loop(0, tb, start, 0)
    lax.fori_loop(0, tb, wait, 0)


def _gather_rows(x, idx, tb):
    n = idx.shape[0]
    width = x.shape[1]
    grid_spec = pltpu.PrefetchScalarGridSpec(
        num_scalar_prefetch=1,
        grid=(n // tb,),
        in_specs=[pl.BlockSpec(memory_space=pl.ANY)],
        out_specs=pl.BlockSpec((tb, width), lambda i, idx_ref: (i, 0)),
        scratch_shapes=[pltpu.SemaphoreType.DMA(())])
    return pl.pallas_call(
        _gather_kernel,
        grid_spec=grid_spec,
        out_shape=jax.ShapeDtypeStruct((n, width), x.dtype),
        compiler_params=_cparams(("arbitrary",)),
        name="gather_rows",
    )(idx, x)


def _moe_kernel(eb_ref, xa_ref, wg_ref, wu_ref, wd_ref, g_ref, b_ref, o_ref, xb_scr, acc_scr):
    i = pl.program_id(0)
    e = pl.program_id(1)

    @pl.when(e == 0)
    def _():
        xb_scr[...] = xa_ref[:, :D_MODEL].astype(BF16)
        acc_scr[...] = jnp.zeros(acc_scr.shape, F32)

    gate = xa_ref[:, D_MODEL:]
    lane = lax.broadcasted_iota(jnp.int32, gate.shape, 1)
    gcol = jnp.sum(jnp.where(lane == ROUTER_E0 + eb_ref[i] + e, gate, 0.0), axis=-1, keepdims=True)

    @pl.when(jnp.max(gcol) > 0.0)
    def _():
        xb = xb_scr[...]
        hg = _dot(xb, wg_ref[...])
        hu = _dot(xb, wu_ref[...])
        h = (hg * _sigmoid(hg)) * hu * gcol
        acc_scr[...] += _dot(h.astype(BF16), wd_ref[...])

    @pl.when(e == pl.num_programs(1) - 1)
    def _():
        o_ref[...] = _layer_norm(ALPHA * xa_ref[:, :D_MODEL] + acc_scr[...], g_ref[...], b_ref[...])


def _moe(xa, expert_base, n_e, wg, wu, wd, ln_g, ln_b, layer, tm):
    m = xa.shape[0]
    wspec = lambda shape: pl.BlockSpec((None, None) + shape, lambda i, e, eb: (layer, eb[i] + e, 0, 0))
    grid_spec = pltpu.PrefetchScalarGridSpec(
        num_scalar_prefetch=1,
        grid=(m // tm, n_e),
        in_specs=[pl.BlockSpec((tm, D_AUG), lambda i, e, eb: (i, 0)),
                  wspec((D_MODEL, D_EXPERT)), wspec((D_MODEL, D_EXPERT)), wspec((D_EXPERT, D_MODEL)),
                  pl.BlockSpec((None, 1, D_MODEL), lambda i, e, eb: (layer, 0, 0)),
                  pl.BlockSpec((None, 1, D_MODEL), lambda i, e, eb: (layer, 0, 0))],
        out_specs=pl.BlockSpec((tm, D_MODEL), lambda i, e, eb: (i, 0)),
        scratch_shapes=[pltpu.VMEM((tm, D_MODEL), BF16), pltpu.VMEM((tm, D_MODEL), F32)])
    return pl.pallas_call(
        _moe_kernel,
        grid_spec=grid_spec,
        out_shape=jax.ShapeDtypeStruct((m, D_MODEL), F32),
        compiler_params=_cparams(("parallel", "arbitrary")),
        name="moe_ln2",
    )(expert_base, xa, wg, wu, wd, ln_g, ln_b)


def _group_sort_plan(xa, tm):
    t = xa.shape[0]
    gi = xa[:, D_MODEL].astype(jnp.int32)
    onehot = (gi[:, None] == jnp.arange(N_GROUPS, dtype=jnp.int32)[None, :]).astype(jnp.int32)
    rank = jnp.sum((jnp.cumsum(onehot, axis=0) - onehot) * onehot, axis=1)
    tiles = (jnp.sum(onehot, axis=0) + tm - 1) // tm
    tile_end = jnp.cumsum(tiles)
    pos = (tile_end - tiles)[gi] * tm + rank
    n_tiles = t // tm + N_GROUPS
    src = jnp.zeros((n_tiles * tm,), jnp.int32).at[pos].set(jnp.arange(t, dtype=jnp.int32))
    tile_group = jnp.sum(jnp.arange(n_tiles, dtype=jnp.int32)[:, None] >= tile_end[None, :], axis=1)
    expert_base = jnp.minimum(tile_group, N_GROUPS - 1).astype(jnp.int32) * EXPERTS_PER_GROUP
    return src, pos.astype(jnp.int32), expert_base


def _pack_w_in(w_in):
    s = 0
    qkv = w_in[..., s:s + C_CONV]; s += C_CONV
    z = w_in[..., s:s + W_BRANCH]; s += W_BRANCH
    beta = w_in[..., s:s + N_HEADS]; s += N_HEADS
    dec = w_in[..., s:s + N_HEADS]; s += N_HEADS
    qb = w_in[..., s:s + W_BRANCH]; s += W_BRANCH
    kb = w_in[..., s:s + W_BRANCH]; s += W_BRANCH
    vb = w_in[..., s:s + W_BRANCH]; s += W_BRANCH
    fb = w_in[..., s:s + N_HEADS]; s += N_HEADS
    ga = w_in[..., s:s + D_MODEL]; s += D_MODEL
    gb = w_in[..., s:s + D_MODEL]; s += D_MODEL
    pad = jnp.zeros(w_in.shape[:-1] + (LANES - 3 * N_HEADS,), w_in.dtype)
    w_main = jnp.concatenate([qkv, z, ga, gb, qb], axis=-1).astype(BF16)
    w_kvs = jnp.concatenate([kb, vb, beta, dec, fb, pad], axis=-1).astype(BF16)
    return w_main, w_kvs


def _lane_row(v, off):
    depth, n = v.shape
    return jnp.zeros((depth, 1, LANES), F32).at[:, 0, off:off + n].set(v.astype(F32))


def _pick_tile(m, pref):
    return pref if m % pref == 0 else m


def kernel(x_prompt, x_sample, cache_k, cache_v, cache_logf, state_conv, state_delta, page_table, w_in, conv_w, a_log, dt_bias, delta_norm_w, fox_forget_b, w_branch_a, w_branch_b, w_out, ln1_g, ln1_b, router_group_w, router_group_b, router_expert_w, router_expert_b, expert_w_gate, expert_w_up, expert_w_down, ln2_g, ln2_b):
    bp, seq, _ = x_prompt.shape
    bs = x_sample.shape[0]
    depth = w_in.shape[0]
    n_pool, _, page, _, _ = cache_k.shape

    w_main, w_kvs = _pack_w_in(w_in)
    wba = w_branch_a.astype(BF16)
    wbb = w_branch_b.astype(BF16)
    wo = w_out.astype(BF16)
    wg = expert_w_gate.astype(BF16)
    wu = expert_w_up.astype(BF16)
    wd = expert_w_down.astype(BF16)
    w_r = jnp.concatenate([router_group_w, router_expert_w,
                           jnp.zeros((depth, D_MODEL, LANES - N_GROUPS - N_EXPERTS), F32)], axis=-1)
    b_r = jnp.concatenate([router_group_b, router_expert_b,
                           jnp.zeros((depth, LANES - N_GROUPS - N_EXPERTS), F32)], axis=-1)[:, None, :]
    a_log_row = _lane_row(a_log, SM_DEC)
    dt_bias_row = _lane_row(dt_bias, SM_DEC)
    fb_row = _lane_row(fox_forget_b, SM_F)
    nw_row = delta_norm_w[:, None, :]
    ln1_g3, ln1_b3, ln2_g3, ln2_b3 = (t[:, None, :] for t in (ln1_g, ln1_b, ln2_g, ln2_b))
    cache_k4 = cache_k.reshape(n_pool, depth, page * N_HEADS, HEAD_DIM)
    cache_v4 = cache_v.reshape(n_pool, depth, page * N_HEADS, HEAD_DIM)
    cache_lf4 = cache_logf.reshape(n_pool, depth, page * N_HEADS // LANES, LANES)

    mp = bp * seq
    hp = x_prompt.reshape(mp, D_MODEL)
    hs = x_sample.reshape(bs, D_MODEL)
    tm_p = _pick_tile(seq, 512)
    tq = _pick_tile(seq, 512)

    k_all = jnp.zeros((bp, depth, seq, W_BRANCH), F32)
    v_all = jnp.zeros((bp, depth, seq, W_BRANCH), F32)
    st_p = []
    st_s = []
    for l in range(depth):
        p = _proj(hp, w_main, l, tm_p, 1536)
        k_all, v_all, sm = _proj_kvs(hp, w_kvs, l, tm_p, bp, seq, depth, l, k_all, v_all)
        oa, s_fin = _delta_prompt(p, sm, conv_w, a_log_row, dt_bias_row, nw_row, l, bp, seq)
        logf, fcum = _logf_prompt(sm, fb_row, l, bp, seq)
        ob = _fox_prompt(p, k_all, v_all, fcum, l, bp, seq, tq)
        x1 = _merge_out(hp, oa, ob, p, wba, wbb, wo, ln1_g3, ln1_b3, l, _pick_tile(mp, 256))
        xa = _router(x1, w_r, b_r, l, tm_p)
        src, pos, expert_base = _group_sort_plan(xa, tm_p)
        xs = _gather_rows(xa, src, _pick_tile(mp, GATHER_ROWS))
        ys = _moe(xs, expert_base, EXPERTS_PER_GROUP, wg, wu, wd, ln2_g3, ln2_b3, l, tm_p)
        hp = _gather_rows(ys, pos, _pick_tile(mp, GATHER_ROWS))
        st_p.append((logf.reshape(bp, seq, LANES)[:, :, SM_F:SM_F + N_HEADS],
                     p.reshape(bp, seq, N_MAIN)[:, seq - (CONV_W - 1):, OFF_QKV:OFF_QKV + C_CONV],
                     s_fin))

        ps = _proj(hs, w_main, l, bs, 1536)
        ks, vs, sms = _proj_kvs(hs, w_kvs, l, bs, 1, bs, 1, 0)
        ks = ks.reshape(bs, N_HEADS, HEAD_DIM)
        vs = vs.reshape(bs, N_HEADS, HEAD_DIM)
        ps3 = ps.reshape(bs, 1, N_MAIN)
        sms3 = sms.reshape(bs, 1, LANES)
        oa_s, s_new = _delta_sample(ps3, sms3, state_conv, conv_w, a_log_row, dt_bias_row, nw_row, state_delta, l)
        q3 = ps[:, OFF_QB:OFF_QB + W_BRANCH].reshape(bs, N_HEADS, HEAD_DIM)
        ob_s, logf_s = _fox_sample(q3, ks, vs, sms3, fb_row, cache_k4, cache_v4, cache_lf4, page_table, l)
        x1s = _merge_out(hs, oa_s.reshape(bs, W_BRANCH), ob_s.reshape(bs, W_BRANCH), ps, wba, wbb, wo,
                         ln1_g3, ln1_b3, l, bs)
        xa_s = _router(x1s, w_r, b_r, l, bs)
        hs = _moe(xa_s, jnp.zeros((1,), jnp.int32), N_EXPERTS, wg, wu, wd, ln2_g3, ln2_b3, l, bs)
        new_buf = jnp.concatenate([state_conv[:, l, 1:], ps3[:, :, OFF_QKV:OFF_QKV + C_CONV]], axis=1)
        st_s.append((ks[:, None], vs[:, None], logf_s[:, :, SM_F:SM_F + N_HEADS], new_buf, s_new))

    outs_p = tuple(jnp.stack([s[i] for s in st_p], axis=1) for i in range(3))
    outs_s = tuple(jnp.stack([s[i] for s in st_s], axis=1) for i in range(5))
    k_prompt = k_all.reshape(bp, depth, seq, N_HEADS, HEAD_DIM)
    v_prompt = v_all.reshape(bp, depth, seq, N_HEADS, HEAD_DIM)
    return (hp.reshape(bp, seq, D_MODEL), hs.reshape(bs, 1, D_MODEL), k_prompt, v_prompt) + outs_p + outs_s
```
